```python
import jax, jax.numpy as jnp
from jax import lax
import numpy as np

D_MODEL = 2048
BATCH = 2
SEQ = 16384
DEPTH = 2

GRID_W = 64
CTX_LEN = 256

RWKV_HEADS = 16
RWKV_HEAD = 64
RWKV_WIDTH = RWKV_HEADS * RWKV_HEAD
DECAY_LORA = 64
AAA_LORA = 64
GATE_LORA = 160
GN_EPS = 64e-5
ATT_Q_HEADS = 16
ATT_KV_HEADS = 4
ATT_HEAD = 64
ATT_GROUP = ATT_Q_HEADS // ATT_KV_HEADS
ATT_WIDTH = ATT_Q_HEADS * ATT_HEAD
ATT_KV_WIDTH = ATT_KV_HEADS * ATT_HEAD
ATT_SCALE = ATT_HEAD ** -0.5
WINDOW = 128
BLOCK = 128
ROPE_THETA = 10000.0
RWKV_COLS = 3 * RWKV_WIDTH + 2 * DECAY_LORA + 2 * AAA_LORA + GATE_LORA
IN_COLS = RWKV_COLS + ATT_WIDTH + 2 * ATT_KV_WIDTH
MIX_WIDTH = RWKV_WIDTH + ATT_WIDTH
POOL_WINDOWS = (2, 4, 8, 16)
POOL_GROUP = D_MODEL // 4
N_EXPERTS = 16
CAPACITY_FACTOR = 2
EXPERT_FF = D_MODEL // 2
NORM_EPS = 1e-6
N_EVEN = (DEPTH + 1) // 2
N_ODD = DEPTH // 2

kernel_name = "hybrid_rwkv7_swa_pool_ecmoe_diffusion"


def rms_norm(x, gain):
    x32 = x.astype(jnp.float32)
    y = x32 * lax.rsqrt(jnp.mean(x32 * x32, axis=-1, keepdims=True) + NORM_EPS)
    return (y * gain.astype(jnp.float32)).astype(x.dtype)


def centred_shift(p):
    zero = jnp.zeros_like(p[:, :1])
    prev = jnp.concatenate([zero, p[:, :-1]], axis=1)
    nxt = jnp.concatenate([p[:, 1:], zero], axis=1)
    return 0.5 * (prev + nxt)


def axial_angles(length):
    rows = length // GRID_W
    row = jnp.repeat(jnp.arange(rows, dtype=jnp.float32), GRID_W)
    col = jnp.tile(jnp.arange(GRID_W, dtype=jnp.float32), rows)
    n_freq = ATT_HEAD // 4
    inv = ROPE_THETA ** (-jnp.arange(n_freq, dtype=jnp.float32) / n_freq)
    return row[:, None] * inv, col[:, None] * inv


def rotate_half(x, ang):
    x1, x2 = jnp.split(x, 2, axis=-1)
    cos = jnp.cos(ang)[None, :, None, :].astype(x.dtype)
    sin = jnp.sin(ang)[None, :, None, :].astype(x.dtype)
    return jnp.concatenate([x1 * cos - x2 * sin, x2 * cos + x1 * sin], axis=-1)


def rope_2d(x, ang_row, ang_col):
    xr, xc = jnp.split(x, 2, axis=-1)
    return jnp.concatenate([rotate_half(xr, ang_row), rotate_half(xc, ang_col)], axis=-1)


def split_heads(t):
    return t.reshape(t.shape[:-1] + (RWKV_HEADS, RWKV_HEAD))


def rwkv_inputs(p, mu):
    m = p + mu * (centred_shift(p) - p)
    sizes = [RWKV_WIDTH] * 3 + [DECAY_LORA] * 2 + [AAA_LORA] * 2 + [GATE_LORA]
    return jnp.split(m, np.cumsum(sizes)[:-1].tolist(), axis=-1)


def rwkv_prep(r, k, v, wd, ad, w0, w2, a0, a2, k_k, k_a):
    w_log = -jax.nn.softplus(-(w0 + jnp.tanh(wd) @ w2).astype(jnp.float32)) - 0.5
    decay = jnp.exp(-jnp.exp(w_log))
    a = jax.nn.sigmoid((a0 + ad @ a2).astype(jnp.float32))
    k32 = k.astype(jnp.float32)
    kk = split_heads(k32 * k_k.astype(jnp.float32))
    kk = kk * lax.rsqrt(jnp.maximum(jnp.sum(kk * kk, axis=-1, keepdims=True), 1e-24))
    k_mod = k32 * (1.0 + (a - 1.0) * k_a.astype(jnp.float32))
    return (split_heads(r.astype(jnp.float32)), split_heads(decay), split_heads(k_mod),
            split_heads(v.astype(jnp.float32)), kk, split_heads(a))


def wkv_step(state, inp):
    r, w, k, v, kk, a = inp
    sa = jnp.einsum('bhvk,bhk->bhv', state, kk)
    state = (state * w[:, :, None, :] - sa[..., None] * (kk * a)[:, :, None, :]
             + v[..., None] * k[:, :, None, :])
    return state, jnp.einsum('bhvk,bhk->bhv', state, r)


def wkv_scan(inputs, state0, reverse):
    xs = tuple(jnp.swapaxes(t, 0, 1) for t in inputs)
    state, ys = lax.scan(wkv_step, state0, xs, reverse=reverse)
    return state, jnp.swapaxes(ys, 0, 1)


def rwkv_bonus(q, r_k_h):
    r, _, k, v, _, _ = q
    return jnp.sum(r * k * r_k_h, axis=-1, keepdims=True) * v


def rwkv_finish(y, bonus, gd, g2, ln_w, ln_b, dtype):
    b, t = y.shape[:2]
    mean = jnp.mean(y, axis=-1, keepdims=True)
    var = jnp.mean(jnp.square(y - mean), axis=-1, keepdims=True)
    yn = ((y - mean) * lax.rsqrt(var + GN_EPS)).reshape(b, t, RWKV_WIDTH)
    yn = yn * ln_w.astype(jnp.float32) + ln_b.astype(jnp.float32)
    g = (jax.nn.sigmoid(gd) @ g2).astype(jnp.float32)
    return ((yn + bonus.reshape(b, t, RWKV_WIDTH)) * g).astype(dtype)


def att_split(p):
    o = RWKV_COLS
    q = p[..., o:o + ATT_WIDTH]
    k = p[..., o + ATT_WIDTH:o + ATT_WIDTH + ATT_KV_WIDTH]
    v = p[..., o + ATT_WIDTH + ATT_KV_WIDTH:]
    sh = p.shape[:2]
    return (q.reshape(sh + (ATT_Q_HEADS, ATT_HEAD)), k.reshape(sh + (ATT_KV_HEADS, ATT_HEAD)),
            v.reshape(sh + (ATT_KV_HEADS, ATT_HEAD)))


def window_attention(q, k, v, kc, vc, sink):
    b, length = q.shape[:2]
    n_ctx = kc.shape[1]
    nb = length // BLOCK
    span = BLOCK + 2 * WINDOW
    qb = q.reshape(b, nb, BLOCK, ATT_KV_HEADS, ATT_GROUP, ATT_HEAD).swapaxes(0, 1)
    pad = ((0, 0), (WINDOW, WINDOW), (0, 0), (0, 0))
    kp = jnp.pad(k, pad)
    vp = jnp.pad(v, pad)
    sink_l = sink.reshape(ATT_KV_HEADS, ATT_GROUP)[None, :, :, None, None].astype(jnp.float32)

    def one_block(args):
        q_blk, n = args
        start = n * BLOCK
        kw = lax.dynamic_slice_in_dim(kp, start, span, axis=1)
        vw = lax.dynamic_slice_in_dim(vp, start, span, axis=1)
        s_c = jnp.einsum('bqkgd,bjkd->bkgqj', q_blk, kc, preferred_element_type=jnp.float32) * ATT_SCALE
        s_w = jnp.einsum('bqkgd,bjkd->bkgqj', q_blk, kw, preferred_element_type=jnp.float32) * ATT_SCALE
        qpos = start + jnp.arange(BLOCK)
        kpos = start - WINDOW + jnp.arange(span)
        ok = ((jnp.abs(qpos[:, None] - kpos[None, :]) <= WINDOW)
              & (kpos >= 0)[None, :] & (kpos < length)[None, :])
        s_w = jnp.where(ok, s_w, -jnp.inf)
        s_sink = jnp.broadcast_to(sink_l, s_c.shape[:-1] + (1,))
        p = jax.nn.softmax(jnp.concatenate([s_c, s_w, s_sink], axis=-1), axis=-1).astype(v.dtype)
        return (jnp.einsum('bkgqj,bjkd->bqkgd', p[..., :n_ctx], vc)
                + jnp.einsum('bkgqj,bjkd->bqkgd', p[..., n_ctx:n_ctx + span], vw))

    out = lax.map(one_block, (qb, jnp.arange(nb)))
    return out.swapaxes(0, 1).reshape(b, length, ATT_WIDTH)


def context_attention(qc, kc, vc, sink):
    b, n_ctx = qc.shape[:2]
    qg = qc.reshape(b, n_ctx, ATT_KV_HEADS, ATT_GROUP, ATT_HEAD)
    s = jnp.einsum('bqkgd,bjkd->bkgqj', qg, kc, preferred_element_type=jnp.float32) * ATT_SCALE
    sink_l = sink.reshape(ATT_KV_HEADS, ATT_GROUP)[None, :, :, None, None].astype(jnp.float32)
    s_sink = jnp.broadcast_to(sink_l, s.shape[:-1] + (1,))
    p = jax.nn.softmax(jnp.concatenate([s, s_sink], axis=-1), axis=-1)[..., :n_ctx].astype(vc.dtype)
    return jnp.einsum('bkgqj,bjkd->bqkgd', p, vc).reshape(b, n_ctx, ATT_WIDTH)


def even_mixer(hx, hc, w_in, mu, w_out, w0, w2, a0, a2, g2, k_k, k_a, r_k, ln_w, ln_b, sink,
               ang_row, ang_col, want_ctx):
    px = hx @ w_in
    pc = hc @ w_in
    rx = rwkv_inputs(px[..., :RWKV_COLS], mu)
    rc = rwkv_inputs(pc[..., :RWKV_COLS], mu)
    zero_state = jnp.zeros((hx.shape[0], RWKV_HEADS, RWKV_HEAD, RWKV_HEAD), jnp.float32)
    r_k_h = r_k.reshape(RWKV_HEADS, RWKV_HEAD).astype(jnp.float32)
    y_x, b_x, y_c, b_c = 0.0, 0.0, 0.0, 0.0
    for d in range(2):
        rev = d == 1
        qx = rwkv_prep(rx[0], rx[1], rx[2], rx[3 + d], rx[5 + d], w0[d], w2[d], a0[d], a2[d], k_k, k_a)
        qc = rwkv_prep(rc[0], rc[1], rc[2], rc[3 + d], rc[5 + d], w0[d], w2[d], a0[d], a2[d], k_k, k_a)
        state_c, yc_d = wkv_scan(qc, zero_state, rev)
        _, yx_d = wkv_scan(qx, state_c, rev)
        y_x = y_x + yx_d
        b_x = b_x + rwkv_bonus(qx, r_k_h)
        if want_ctx:
            y_c = y_c + yc_d
            b_c = b_c + rwkv_bonus(qc, r_k_h)
    rwkv_x = rwkv_finish(y_x, b_x, rx[7], g2, ln_w, ln_b, hx.dtype)

    q_x, k_x, v_x = att_split(px)
    q_c, k_c, v_c = att_split(pc)
    q_x = rope_2d(q_x, ang_row, ang_col)
    k_x = rope_2d(k_x, ang_row, ang_col)
    att_x = window_attention(q_x, k_x, v_x, k_c, v_c, sink)
    out_x = jnp.concatenate([rwkv_x, att_x], axis=-1) @ w_out
    if not want_ctx:
        return out_x, None
    rwkv_c = rwkv_finish(y_c, b_c, rc[7], g2, ln_w, ln_b, hc.dtype)
    att_c = context_attention(q_c, k_c, v_c, sink)
    return out_x, jnp.concatenate([rwkv_c, att_c], axis=-1) @ w_out


def pool_mixer(h, pool_w, pool_scale):
    b, t, d = h.shape
    h32 = h.astype(jnp.float32)
    cs = jnp.concatenate([jnp.zeros((b, 1, d), jnp.float32), jnp.cumsum(h32, axis=1)], axis=1)
    pos = jnp.arange(t)
    outs = []
    for g, w in enumerate(POOL_WINDOWS):
        sl = slice(g * POOL_GROUP, (g + 1) * POOL_GROUP)
        lo = jnp.clip(pos - w // 2, 0, t)
        hi = jnp.clip(pos + w // 2, 0, t)
        cnt = (hi - lo).astype(jnp.float32)[None, :, None]
        cs_g = cs[..., sl]
        mean = (cs_g[:, hi] - cs_g[:, lo]) / cnt
        outs.append((mean - h32[..., sl]).astype(h.dtype) @ pool_w[g])
    return jnp.concatenate(outs, axis=-1) * pool_scale


def expert_choice_ffn(h, router_w, w_gate, w_up, w_down):
    b, t, _ = h.shape
    cap = max(1, CAPACITY_FACTOR * t // N_EXPERTS)
    aff = jax.nn.softmax(jnp.einsum('btd,de->bte', h, router_w, preferred_element_type=jnp.float32), axis=-1)
    gate, idx = lax.top_k(jnp.swapaxes(aff, 1, 2), cap)
    b_idx = jnp.arange(b)[:, None, None]
    xs = h[b_idx, idx]
    hid = jax.nn.silu(jnp.einsum('becd,edf->becf', xs, w_gate)) * jnp.einsum('becd,edf->becf', xs, w_up)
    y = jnp.einsum('becf,efd->becd', hid, w_down) * gate[..., None].astype(h.dtype)
    return jnp.zeros_like(h).at[b_idx, idx].add(y)


def setup_inputs(seed: int = 0) -> dict:
    key = jax.random.key(seed)
    ks = iter(jax.random.split(key, 40))
    D = D_MODEL

    def nrm(shape, s):
        return jax.random.normal(next(ks), shape, jnp.float32) * s

    def unif(shape, lo, hi):
        return jax.random.uniform(next(ks), shape, jnp.float32, minval=lo, maxval=hi)

    return {
        "x": nrm((BATCH, SEQ, D), 1.0),
        "c": nrm((BATCH, D), 1.0),
        "ctx": nrm((BATCH, CTX_LEN, D), 1.0),
        "c_ctx": nrm((D,), 1.0),
        "ada_w": nrm((DEPTH, D, 6 * D), 0.5 * D ** -0.5),
        "ada_b": nrm((DEPTH, 6 * D), 0.02),
        "norm_mix": 1.0 + nrm((DEPTH, D), 0.01),
        "norm_ffn": 1.0 + nrm((DEPTH, D), 0.01),
        "w_in": nrm((N_EVEN, D, IN_COLS), D ** -0.5),
        "shift_mu": unif((N_EVEN, RWKV_COLS), 0.0, 1.0),
        "decay_w0": unif((N_EVEN, 2, RWKV_WIDTH), -5.0, 1.0),
        "decay_w2": nrm((N_EVEN, 2, DECAY_LORA, RWKV_WIDTH), 0.5 * DECAY_LORA ** -0.5),
        "iclr_a0": nrm((N_EVEN, 2, RWKV_WIDTH), 0.5),
        "iclr_a2": nrm((N_EVEN, 2, AAA_LORA, RWKV_WIDTH), 0.5 * AAA_LORA ** -0.5),
        "gate_g2": nrm((N_EVEN, GATE_LORA, RWKV_WIDTH), GATE_LORA ** -0.5),
        "k_k": 0.85 + nrm((N_EVEN, RWKV_WIDTH), 0.05),
        "k_a": 1.0 + nrm((N_EVEN, RWKV_WIDTH), 0.05),
        "r_k": nrm((N_EVEN, RWKV_WIDTH), 0.1),
        "ln_w": 1.0 + nrm((N_EVEN, RWKV_WIDTH), 0.05),
        "ln_b": nrm((N_EVEN, RWKV_WIDTH), 0.01),
        "sink": nrm((N_EVEN, ATT_Q_HEADS), 1.0),
        "w_out": nrm((N_EVEN, MIX_WIDTH, D), MIX_WIDTH ** -0.5),
        "pool_w": nrm((N_ODD, len(POOL_WINDOWS), POOL_GROUP, POOL_GROUP), POOL_GROUP ** -0.5),
        "pool_scale": 1.0 + nrm((N_ODD, D), 0.1),
        "router_w": nrm((DEPTH, D, N_EXPERTS), D ** -0.5),
        "exp_w_gate": nrm((DEPTH, N_EXPERTS, D, EXPERT_FF), D ** -0.5),
        "exp_w_up": nrm((DEPTH, N_EXPERTS, D, EXPERT_FF), D ** -0.5),
        "exp_w_down": nrm((DEPTH, N_EXPERTS, EXPERT_FF, D), EXPERT_FF ** -0.5),
        "norm_final": 1.0 + nrm((D,), 0.01),
    }


def reference(x, c, ctx, c_ctx, ada_w, ada_b, norm_mix, norm_ffn, w_in, shift_mu, decay_w0, decay_w2,
              iclr_a0, iclr_a2, gate_g2, k_k, k_a, r_k, ln_w, ln_b, sink, w_out, pool_w, pool_scale,
              router_w, exp_w_gate, exp_w_up, exp_w_down, norm_final):
    ang_row, ang_col = axial_angles(x.shape[1])
    for l in range(DEPTH):
        want_ctx = any(j % 2 == 0 for j in range(l + 1, DEPTH))
        sh_a, sc_a, gt_a, sh_f, sc_f, gt_f = jnp.split(
            (jax.nn.silu(c) @ ada_w[l] + ada_b[l])[:, None, :], 6, axis=-1)
        hx = rms_norm(x, norm_mix[l]) * (1.0 + sc_a) + sh_a
        if l % 2 == 0 or want_ctx:
            csh_a, csc_a, cgt_a, csh_f, csc_f, cgt_f = jnp.split(
                jax.nn.silu(c_ctx) @ ada_w[l] + ada_b[l], 6, axis=-1)
            hc = rms_norm(ctx, norm_mix[l]) * (1.0 + csc_a) + csh_a
        if l % 2 == 0:
            e = l // 2
            yx, yc = even_mixer(hx, hc, w_in[e], shift_mu[e], w_out[e], decay_w0[e], decay_w2[e],
                                iclr_a0[e], iclr_a2[e], gate_g2[e], k_k[e], k_a[e], r_k[e], ln_w[e],
                                ln_b[e], sink[e], ang_row, ang_col, want_ctx)
        else:
            o = l // 2
            yx = pool_mixer(hx, pool_w[o], pool_scale[o])
            yc = pool_mixer(hc, pool_w[o], pool_scale[o]) if want_ctx else None
        x = x + gt_a * yx
        x = x + gt_f * expert_choice_ffn(rms_norm(x, norm_ffn[l]) * (1.0 + sc_f) + sh_f,
                                         router_w[l], exp_w_gate[l], exp_w_up[l], exp_w_down[l])
        if want_ctx:
            ctx = ctx + cgt_a * yc
            ctx = ctx + cgt_f * expert_choice_ffn(rms_norm(ctx, norm_ffn[l]) * (1.0 + csc_f) + csh_f,
                                                  router_w[l], exp_w_gate[l], exp_w_up[l], exp_w_down[l])
    return rms_norm(x, norm_final)
```

```python
import functools

import jax
import jax.numpy as jnp
from jax import lax
from jax.experimental import pallas as pl
from jax.experimental.pallas import tpu as pltpu

F32 = jnp.float32
BF16 = jnp.bfloat16
HIGHEST = lax.Precision.HIGHEST

GRID_W = 64
HEADS = 16
HEAD = 64
RW = HEADS * HEAD
DECAY_LORA = 64
AAA_LORA = 64
GATE_LORA = 160
GN_EPS = 64e-5
KV_HEADS = 4
GROUP = HEADS // KV_HEADS
KVW = KV_HEADS * HEAD
ATT_SCALE = HEAD ** -0.5
WINDOW = 128
BLOCK = 128
ROPE_THETA = 10000.0
POOL_WINDOWS = (2, 4, 8, 16)
N_EXPERTS = 16
CAPACITY_FACTOR = 2
NORM_EPS = 1e-6

COL_RKV = 0
COL_Q = 3 * RW
COL_AK = 4 * RW
COL_AV = 4 * RW + KVW
COL_LORA = 4 * RW + 2 * KVW
LORA_W = 512
N_COLS = COL_LORA + LORA_W
LORA_GATE = 2 * DECAY_LORA + 2 * AAA_LORA
CHUNK = 64
HALO = 8
LANES = 128
VMEM_LIMIT = 56 * 1024 * 1024


def _cparams(sem):
    return pltpu.CompilerParams(dimension_semantics=sem, vmem_limit_bytes=VMEM_LIMIT)


def _tile(n, pref):
    t = min(n, pref)
    assert n % t == 0, (n, t)
    return t


def _bdot(a, b):
    return jnp.dot(a.astype(BF16), b.astype(BF16), preferred_element_type=F32)


def _bdot_nt(a, b):
    return lax.dot_general(a.astype(BF16), b.astype(BF16), (((1,), (1,)), ((), ())),
                           preferred_element_type=F32)


def _fdot(a, b):
    return jnp.dot(a, b, preferred_element_type=F32, precision=HIGHEST)


def _rms_mod(x, g, sc, sh):
    y = x * lax.rsqrt(jnp.mean(x * x, axis=-1, keepdims=True) + NORM_EPS)
    return (y * g) * (1.0 + sc) + sh


def _head_sum(x, bd):
    return jnp.concatenate(
        [_fdot(x[:, g * LANES:(g + 1) * LANES], bd) for g in range(x.shape[1] // LANES)], axis=1)


def _block_diag_ones():
    r = lax.broadcasted_iota(jnp.int32, (LANES, LANES), 0) // HEAD
    c = lax.broadcasted_iota(jnp.int32, (LANES, LANES), 1) // HEAD
    return (r == c).astype(F32)


def _ada_kernel(cc_ref, w_ref, b_ref, o_ref):
    cc = cc_ref[...]
    s = cc * jax.nn.sigmoid(cc)
    o_ref[...] = _bdot(s, w_ref[...]) + b_ref[...]


def _ada(cc, ada_w, ada_b):
    n_l, d, n = ada_w.shape
    tn = _tile(n, 1536)
    return pl.pallas_call(
        _ada_kernel,
        grid=(n_l, n // tn),
        in_specs=[pl.BlockSpec((8, d), lambda l, j: (0, 0)),
                  pl.BlockSpec((None, d, tn), lambda l, j: (l, 0, j)),
                  pl.BlockSpec((None, 1, tn), lambda l, j: (l, 0, j))],
        out_specs=pl.BlockSpec((None, 8, tn), lambda l, j: (l, 0, j)),
        out_shape=jax.ShapeDtypeStruct((n_l, 8, n), F32),
        compiler_params=_cparams(("arbitrary", "arbitrary")),
        name="adaln",
    )(cc, ada_w, ada_b.reshape(n_l, 1, n))


def _inproj_kernel(*refs, rope_lo, rope_hi, tn):
    if rope_hi > rope_lo:
        x_ref, g_ref, sc_ref, sh_ref, w_ref, cos_ref, sin_ref, o_ref, h_scr = refs
    else:
        x_ref, g_ref, sc_ref, sh_ref, w_ref, o_ref, h_scr = refs
    j = pl.program_id(2)

    @pl.when(j == 0)
    def _():
        h_scr[...] = _rms_mod(x_ref[...], g_ref[...], sc_ref[...], sh_ref[...]).astype(BF16)

    acc = jnp.dot(h_scr[...], w_ref[...], preferred_element_type=F32)
    if rope_hi > rope_lo:
        is_rope = (j >= rope_lo) & (j < rope_hi)

        @pl.when(is_rope)
        def _():
            cos = cos_ref[...]
            sin = sin_ref[...]
            lane = lax.broadcasted_iota(jnp.int32, cos.shape, 1)
            first = (lane % 32) < 16
            outs = []
            for g in range(tn // LANES):
                a = acc[:, g * LANES:(g + 1) * LANES]
                partner = jnp.where(first, pltpu.roll(a, LANES - 16, 1), pltpu.roll(a, 16, 1))
                outs.append(a * cos + partner * sin)
            o_ref[...] = jnp.concatenate(outs, axis=1)

        @pl.when(jnp.logical_not(is_rope))
        def _():
            o_ref[...] = acc
    else:
        o_ref[...] = acc


def _inproj(x, g, sc, sh, w, cos_t=None, sin_t=None):
    b, t, d = x.shape
    n = w.shape[1]
    tm = _tile(t, 1024)
    tn = 256
    rope = cos_t is not None
    rope_lo, rope_hi = (COL_Q // tn, COL_AV // tn) if rope else (0, 0)
    in_specs = [pl.BlockSpec((None, tm, d), lambda bi, i, j: (bi, i, 0)),
                pl.BlockSpec((1, d), lambda bi, i, j: (0, 0)),
                pl.BlockSpec((None, 1, d), lambda bi, i, j: (bi, 0, 0)),
                pl.BlockSpec((None, 1, d), lambda bi, i, j: (bi, 0, 0)),
                pl.BlockSpec((d, tn), lambda bi, i, j: (0, j))]
    args = [x, g.reshape(1, d), sc, sh, w]
    if rope:
        in_specs += [pl.BlockSpec((tm, LANES), lambda bi, i, j: (i, 0)),
                     pl.BlockSpec((tm, LANES), lambda bi, i, j: (i, 0))]
        args += [cos_t, sin_t]
    return pl.pallas_call(
        functools.partial(_inproj_kernel, rope_lo=rope_lo, rope_hi=rope_hi, tn=tn),
        grid=(b, t // tm, n // tn),
        in_specs=in_specs,
        out_specs=pl.BlockSpec((None, tm, tn), lambda bi, i, j: (bi, i, j)),
        out_shape=jax.ShapeDtypeStruct((b, t, n), F32),
        scratch_shapes=[pltpu.VMEM((tm, d), BF16)],
        compiler_params=_cparams(("arbitrary", "arbitrary", "arbitrary")),
        name="inproj_rope" if rope else "inproj_ctx",
    )(*args)


def _shift_lerp(p, prev_blk, next_blk, mu, has_prev, has_next):
    rows = p.shape[0]
    ridx = lax.broadcasted_iota(jnp.int32, p.shape, 0)
    prev_row = jnp.where(has_prev, prev_blk[HALO - 1:HALO, :], 0.0)
    next_row = jnp.where(has_next, next_blk[0:1, :], 0.0)
    prev = jnp.where(ridx == 0, prev_row, pltpu.roll(p, 1, 0))
    nxt = jnp.where(ridx == rows - 1, next_row, pltpu.roll(p, rows - 1, 0))
    return p + mu * (0.5 * (prev + nxt) - p)


def _rwkv_kernel(*refs, n_chunks, seq_len):
    (rkv_f, lo_f, rkv_fp, lo_fp, rkv_fn, lo_fn,
     rkv_b, lo_b, rkv_bp, lo_bp, rkv_bn, lo_bn,
     mu_rkv_ref, mu_lo_ref, w0_ref, w2_ref, a0_ref, a2_ref, kk_ref, ka_ref, rk_ref, s0_ref,
     yf_ref, yb_ref, bf_ref, bb_ref, sfin_ref, state) = refs
    c = pl.program_id(1)
    cn = CHUNK

    @pl.when(c == 0)
    def _():
        state[...] = s0_ref[...]

    bd = _block_diag_ones()
    ri = lax.broadcasted_iota(jnp.int32, (cn, cn), 0)
    ci = lax.broadcasted_iota(jnp.int32, (cn, cn), 1)
    eye = (ri == ci).astype(F32)
    mu_rkv = mu_rkv_ref[...]
    mu_lo = mu_lo_ref[...]
    k_k = kk_ref[...]
    k_a = ka_ref[...]
    r_k = rk_ref[...]

    dirs = ((0, c, rkv_f, lo_f, rkv_fp, lo_fp, rkv_fn, lo_fn, yf_ref, bf_ref),
            (1, n_chunks - 1 - c, rkv_b, lo_b, rkv_bp, lo_bp, rkv_bn, lo_bn, yb_ref, bb_ref))
    for d, cd, rkv_ref, lo_ref, rkv_p, lo_p, rkv_n, lo_n, y_ref, bon_ref in dirs:
        has_prev = cd > 0
        has_next = cd < n_chunks - 1
        m = _shift_lerp(rkv_ref[...], rkv_p[...], rkv_n[...], mu_rkv, has_prev, has_next)
        ml = _shift_lerp(lo_ref[...], lo_p[...], lo_n[...], mu_lo, has_prev, has_next)
        r = m[:, 0:RW]
        k = m[:, RW:2 * RW]
        v = m[:, 2 * RW:3 * RW]
        wd = ml[:, d * DECAY_LORA:(d + 1) * DECAY_LORA]
        ad = ml[:, 2 * DECAY_LORA + d * AAA_LORA:2 * DECAY_LORA + (d + 1) * AAA_LORA]
        w_pre = w0_ref[d] + _bdot(jnp.tanh(wd), w2_ref[d])
        z = -w_pre
        softplus = jnp.maximum(z, 0.0) + jnp.log(1.0 + jnp.exp(-jnp.abs(z)))
        logw = -jnp.exp(-softplus - 0.5)
        a = jax.nn.sigmoid(a0_ref[d] + _bdot(ad, a2_ref[d]))
        kk = k * k_k
        kk = kk * lax.rsqrt(jnp.maximum(_head_sum(kk * kk, bd), 1e-24))
        k_mod = k * (1.0 + (a - 1.0) * k_a)
        bon_ref[...] = _head_sum(r * k_mod * r_k, bd) * v

        if d == 0:
            tri = (ci <= ri).astype(F32)
            strict = ci < ri
            incl = ci <= ri
            last = cn - 1
        else:
            tri = (ci >= ri).astype(F32)
            strict = ci > ri
            incl = ci >= ri
            last = 0
        cum = _fdot(tri, logw)
        e_pos = jnp.exp(cum)
        e_neg = jnp.exp(-cum)
        p_last = e_pos[last:last + 1, :]
        at = (-kk) * jnp.exp(cum - logw)
        rt = r * e_pos
        bh = (kk * a) * e_neg
        kh = k_mod * e_neg
        bbar = (bh * p_last).astype(BF16)
        kbar = (kh * p_last).astype(BF16)
        at_b = at.astype(BF16)
        rt_b = rt.astype(BF16)
        bh_b = bh.astype(BF16)
        kh_b = kh.astype(BF16)
        v_b = v.astype(BF16)

        ys = []
        for hp in range(HEADS // 2):
            us = []
            for h in (2 * hp, 2 * hp + 1):
                sl = slice(h * HEAD, (h + 1) * HEAD)
                g = _bdot_nt(jnp.concatenate([at_b[:, sl], rt_b[:, sl]], axis=0),
                             jnp.concatenate([bh_b[:, sl], kh_b[:, sl]], axis=0))
                a_ab = jnp.where(strict, g[:cn, :cn], 0.0)
                a_ak = jnp.where(strict, g[:cn, cn:], 0.0)
                a_rb = jnp.where(incl, g[cn:, :cn], 0.0)
                a_rk = jnp.where(incl, g[cn:, cn:], 0.0)
                p = a_ab
                tinv = eye + p
                for _ in range(5):
                    p = _bdot(p, p)
                    tinv = tinv + _bdot(tinv, p)
                av = _bdot(jnp.concatenate([a_ak, a_rk], axis=0), v_b[:, sl])
                w12 = _bdot(tinv, jnp.concatenate([at[:, sl], av[:cn]], axis=1))
                s0 = state[d, h]
                us0 = _bdot_nt(jnp.concatenate([w12[:, :HEAD], rt[:, sl]], axis=0), s0)
                u = us0[:cn] + w12[:, HEAD:]
                ys.append(us0[cn:] + _bdot(a_rb, u) + av[cn:])
                us.append(u)
            uv = jnp.concatenate([jnp.concatenate(us, axis=1),
                                  v[:, 2 * hp * HEAD:(2 * hp + 2) * HEAD]], axis=0)
            uvt = uv.T
            for i, h in enumerate((2 * hp, 2 * hp + 1)):
                sl = slice(h * HEAD, (h + 1) * HEAD)
                upd = _bdot(uvt[i * HEAD:(i + 1) * HEAD, :],
                            jnp.concatenate([bbar[:, sl], kbar[:, sl]], axis=0))
                state[d, h] = state[d, h] * p_last[:, sl] + upd
        y_ref[...] = jnp.concatenate(ys, axis=1)

    @pl.when(c == n_chunks - 1)
    def _():
        sfin_ref[...] = state[...]


def _rwkv_scan(px, s0, mu_rkv, mu_lo, w0, w2, a0, a2, k_k, k_a, r_k):
    b, t, _ = px.shape
    cn = CHUNK
    nc = t // cn
    assert t % cn == 0 and cn % HALO == 0
    hb = cn // HALO
    n_halo = t // HALO
    lora_blk = COL_LORA // LORA_W

    def fwd(ci):
        return ci

    def bwd(ci):
        return nc - 1 - ci

    def specs(cmap):
        return [
            pl.BlockSpec((None, cn, 3 * RW), lambda bi, ci: (bi, cmap(ci), 0)),
            pl.BlockSpec((None, cn, LORA_W), lambda bi, ci: (bi, cmap(ci), lora_blk)),
            pl.BlockSpec((None, HALO, 3 * RW), lambda bi, ci: (bi, jnp.maximum(cmap(ci) * hb - 1, 0), 0)),
            pl.BlockSpec((None, HALO, LORA_W),
                         lambda bi, ci: (bi, jnp.maximum(cmap(ci) * hb - 1, 0), lora_blk)),
            pl.BlockSpec((None, HALO, 3 * RW),
                         lambda bi, ci: (bi, jnp.minimum((cmap(ci) + 1) * hb, n_halo - 1), 0)),
            pl.BlockSpec((None, HALO, LORA_W),
                         lambda bi, ci: (bi, jnp.minimum((cmap(ci) + 1) * hb, n_halo - 1), lora_blk)),
        ]

    def const(shape):
        nd = len(shape)
        return pl.BlockSpec(shape, lambda bi, ci: (0,) * nd)

    in_specs = specs(fwd) + specs(bwd) + [
        const((1, 3 * RW)), const((1, LORA_W)),
        const((2, 1, RW)), const((2, DECAY_LORA, RW)), const((2, 1, RW)), const((2, AAA_LORA, RW)),
        const((1, RW)), const((1, RW)), const((1, RW)),
        pl.BlockSpec((None, 2, HEADS, HEAD, HEAD), lambda bi, ci: (bi, 0, 0, 0, 0)),
    ]
    yshape = jax.ShapeDtypeStruct((b, t, RW), F32)
    out_specs = [
        pl.BlockSpec((None, cn, RW), lambda bi, ci: (bi, ci, 0)),
        pl.BlockSpec((None, cn, RW), lambda bi, ci: (bi, nc - 1 - ci, 0)),
        pl.BlockSpec((None, cn, RW), lambda bi, ci: (bi, ci, 0)),
        pl.BlockSpec((None, cn, RW), lambda bi, ci: (bi, nc - 1 - ci, 0)),
        pl.BlockSpec((None, 2, HEADS, HEAD, HEAD), lambda bi, ci: (bi, 0, 0, 0, 0)),
    ]
    px_args = [px] * 12
    return pl.pallas_call(
        functools.partial(_rwkv_kernel, n_chunks=nc, seq_len=t),
        grid=(b, nc),
        in_specs=in_specs,
        out_specs=out_specs,
        out_shape=[yshape, yshape, yshape, yshape,
                   jax.ShapeDtypeStruct((b, 2, HEADS, HEAD, HEAD), F32)],
        scratch_shapes=[pltpu.VMEM((2, HEADS, HEAD, HEAD), F32)],
        compiler_params=_cparams(("arbitrary", "arbitrary")),
        name="rwkv_scan",
    )(*px_args, mu_rkv, mu_lo, w0, w2.astype(BF16), a0, a2.astype(BF16), k_k, k_a, r_k, s0)


def _relayout_in_cols(a):
    n_lora = 2 * DECAY_LORA + 2 * AAA_LORA + GATE_LORA
    rkv = a[..., :3 * RW]
    lora = a[..., 3 * RW:3 * RW + n_lora]
    att = a[..., 3 * RW + n_lora:]
    pad = jnp.zeros(a.shape[:-1] + (LORA_W - n_lora,), a.dtype)
    return jnp.concatenate([rkv, att, lora, pad], axis=-1)


def _rope_tables(t):
    rows = t // GRID_W
    row = jnp.repeat(jnp.arange(rows, dtype=F32), GRID_W)
    col = jnp.tile(jnp.arange(GRID_W, dtype=F32), rows)
    n_freq = HEAD // 4
    inv = ROPE_THETA ** (-jnp.arange(n_freq, dtype=F32) / n_freq)
    ar = row[:, None] * inv
    ac = col[:, None] * inv
    cos = jnp.concatenate([jnp.cos(ar), jnp.cos(ar), jnp.cos(ac), jnp.cos(ac)], axis=-1)
    sin = jnp.concatenate([-jnp.sin(ar), jnp.sin(ar), -jnp.sin(ac), jnp.sin(ac)], axis=-1)
    reps = LANES // HEAD
    return jnp.tile(cos, (1, reps)), jnp.tile(sin, (1, reps))


def _finish_kernel(yf_ref, yb_ref, bf_ref, bb_ref, lo_ref, lo_p, lo_n, mu_lo_ref, g2_ref, lnw_ref, lnb_ref,
                   o_ref, *, n_tiles):
    i = pl.program_id(1)
    bd = _block_diag_ones()
    y = yf_ref[...] + yb_ref[...]
    mean = _head_sum(y, bd) * (1.0 / HEAD)
    yc = y - mean
    var = _head_sum(yc * yc, bd) * (1.0 / HEAD)
    yn = yc * lax.rsqrt(var + GN_EPS) * lnw_ref[...] + lnb_ref[...]
    ml = _shift_lerp(lo_ref[...], lo_p[...], lo_n[...], mu_lo_ref[...], i > 0, i < n_tiles - 1)
    gate = _bdot(jax.nn.sigmoid(ml), g2_ref[...])
    o_ref[...] = ((yn + bf_ref[...] + bb_ref[...]) * gate).astype(o_ref.dtype)


def _rwkv_finish(yf, yb, bf, bb, px, mu_lo, g2_pad, ln_w, ln_b):
    b, t, _ = yf.shape
    tm = _tile(t, 256)
    nt = t // tm
    hb = tm // HALO
    n_halo = t // HALO
    lora_blk = COL_LORA // LORA_W
    tile = pl.BlockSpec((None, tm, RW), lambda bi, i: (bi, i, 0))
    vec = pl.BlockSpec((1, RW), lambda bi, i: (0, 0))
    return pl.pallas_call(
        functools.partial(_finish_kernel, n_tiles=nt),
        grid=(b, nt),
        in_specs=[tile, tile, tile, tile,
                  pl.BlockSpec((None, tm, LORA_W), lambda bi, i: (bi, i, lora_blk)),
                  pl.BlockSpec((None, HALO, LORA_W), lambda bi, i: (bi, jnp.maximum(i * hb - 1, 0), lora_blk)),
                  pl.BlockSpec((None, HALO, LORA_W),
                               lambda bi, i: (bi, jnp.minimum((i + 1) * hb, n_halo - 1), lora_blk)),
                  pl.BlockSpec((1, LORA_W), lambda bi, i: (0, 0)),
                  pl.BlockSpec((LORA_W, RW), lambda bi, i: (0, 0)),
                  vec, vec],
        out_specs=tile,
        out_shape=jax.ShapeDtypeStruct((b, t, RW), BF16),
        compiler_params=_cparams(("arbitrary", "arbitrary")),
        name="rwkv_finish",
    )(yf, yb, bf, bb, px, px, px, mu_lo, g2_pad, ln_w, ln_b)


def _attn_kernel(q_ref, kp_ref, kc_ref, kn_ref, vp_ref, vc_ref, vn_ref, ck_ref, cv_ref, sink_ref, o_ref,
                 *, n_blocks, n_ctx):
    n = pl.program_id(1)
    q = q_ref[...]
    k_win = jnp.concatenate([kp_ref[...], kc_ref[...], kn_ref[...]], axis=0)
    v_win = jnp.concatenate([vp_ref[...], vc_ref[...], vn_ref[...]], axis=0)
    k_all = jnp.concatenate([ck_ref[...], k_win], axis=0).astype(BF16)
    v_all = jnp.concatenate([cv_ref[...], v_win], axis=0).astype(BF16)
    rows = GROUP * BLOCK
    span = n_ctx + 3 * BLOCK
    qi = lax.broadcasted_iota(jnp.int32, (rows, span), 0) % BLOCK + BLOCK
    kj = lax.broadcasted_iota(jnp.int32, (rows, span), 1) - n_ctx
    k_lo = jnp.where(n == 0, BLOCK, 0)
    k_hi = jnp.where(n == n_blocks - 1, 2 * BLOCK, 3 * BLOCK)
    in_win = (jnp.abs(qi - kj) <= WINDOW) & (kj >= k_lo) & (kj < k_hi)
    ok = (kj < 0) | in_win
    sink = sink_ref[...]
    outs = []
    for g in range(KV_HEADS):
        qg = jnp.concatenate([q[:, (g * GROUP + i) * HEAD:(g * GROUP + i + 1) * HEAD] for i in range(GROUP)],
                             axis=0)
        kg = k_all[:, g * HEAD:(g + 1) * HEAD]
        vg = v_all[:, g * HEAD:(g + 1) * HEAD]
        s = _bdot_nt(qg, kg) * ATT_SCALE
        s = jnp.where(ok, s, -jnp.inf)
        sk = jnp.concatenate([jnp.broadcast_to(sink[:, g * GROUP + i:g * GROUP + i + 1], (BLOCK, 1))
                              for i in range(GROUP)], axis=0)
        m = jnp.maximum(jnp.max(s, axis=-1, keepdims=True), sk)
        p = jnp.exp(s - m)
        denom = jnp.sum(p, axis=-1, keepdims=True) + jnp.exp(sk - m)
        og = _bdot(p, vg) / denom
        outs += [og[i * BLOCK:(i + 1) * BLOCK] for i in range(GROUP)]
    o_ref[...] = jnp.concatenate(outs, axis=1).astype(o_ref.dtype)


def _attention(px, pc, sink):
    b, t, _ = px.shape
    n_ctx = pc.shape[1]
    nb = t // BLOCK
    kb = COL_AK // KVW
    vb = COL_AV // KVW

    def blk(col, off):
        return pl.BlockSpec((None, BLOCK, KVW), lambda bi, n: (bi, jnp.clip(n + off, 0, nb - 1), col))

    return pl.pallas_call(
        functools.partial(_attn_kernel, n_blocks=nb, n_ctx=n_ctx),
        grid=(b, nb),
        in_specs=[pl.BlockSpec((None, BLOCK, RW), lambda bi, n: (bi, n, COL_Q // RW)),
                  blk(kb, -1), blk(kb, 0), blk(kb, 1), blk(vb, -1), blk(vb, 0), blk(vb, 1),
                  pl.BlockSpec((None, n_ctx, KVW), lambda bi, n: (bi, 0, kb)),
                  pl.BlockSpec((None, n_ctx, KVW), lambda bi, n: (bi, 0, vb)),
                  pl.BlockSpec((1, HEADS), lambda bi, n: (0, 0))],
        out_specs=pl.BlockSpec((None, BLOCK, RW), lambda bi, n: (bi, n, 0)),
        out_shape=jax.ShapeDtypeStruct((b, t, RW), BF16),
        compiler_params=_cparams(("arbitrary", "arbitrary")),
        name="window_attention",
    )(px, px, px, px, px, px, px, pc, pc, sink.reshape(1, HEADS))


def _ffn_pre(x_new, gf_ref, scf_ref, shf_ref, rw_ref, x_out, hf_out, aff_out):
    x_out[...] = x_new
    hf = _rms_mod(x_new, gf_ref[...], scf_ref[...], shf_ref[...])
    hf_out[...] = hf
    logits = _bdot(hf, rw_ref[...])
    e = jnp.exp(logits - jnp.max(logits, axis=-1, keepdims=True))
    aff_out[...] = e / jnp.sum(e, axis=-1, keepdims=True)


def _outproj_kernel(ra_ref, at_ref, w_ref, x_ref, gt_ref, gf_ref, scf_ref, shf_ref, rw_ref,
                    x_out, hf_out, aff_out):
    y = (jnp.dot(ra_ref[...], w_ref[0:RW, :], preferred_element_type=F32)
         + jnp.dot(at_ref[...], w_ref[RW:2 * RW, :], preferred_element_type=F32))
    _ffn_pre(x_ref[...] + gt_ref[...] * y, gf_ref, scf_ref, shf_ref, rw_ref, x_out, hf_out, aff_out)


def _mod_specs(d):
    return pl.BlockSpec((None, 1, d), lambda bi, i: (bi, 0, 0))


def _outproj(rwkv_x, att_x, w_out, x, gt_a, g_ffn, sc_f, sh_f, router_w):
    b, t, d = x.shape
    ne = router_w.shape[1]
    tm = _tile(t, 256)
    xt = pl.BlockSpec((None, tm, d), lambda bi, i: (bi, i, 0))
    mt = pl.BlockSpec((None, tm, RW), lambda bi, i: (bi, i, 0))
    return pl.pallas_call(
        _outproj_kernel,
        grid=(b, t // tm),
        in_specs=[mt, mt, pl.BlockSpec((2 * RW, d), lambda bi, i: (0, 0)), xt, _mod_specs(d),
                  pl.BlockSpec((1, d), lambda bi, i: (0, 0)), _mod_specs(d), _mod_specs(d),
                  pl.BlockSpec((d, ne), lambda bi, i: (0, 0))],
        out_specs=[xt, xt, pl.BlockSpec((None, tm, ne), lambda bi, i: (bi, i, 0))],
        out_shape=[jax.ShapeDtypeStruct((b, t, d), F32), jax.ShapeDtypeStruct((b, t, d), F32),
                   jax.ShapeDtypeStruct((b, t, ne), F32)],
        compiler_params=_cparams(("arbitrary", "arbitrary")),
        name="outproj_ffnpre",
    )(rwkv_x, att_x, w_out, x, gt_a, g_ffn.reshape(1, d), sc_f, sh_f, router_w)


def _pool_kernel(x_ref, xp_ref, xn_ref, g_ref, sc_ref, sh_ref, pw_ref, ps_ref, gt_ref,
                 gf_ref, scf_ref, shf_ref, rw_ref, x_out, hf_out, aff_out, ext, *, tm, seq_len):
    i = pl.program_id(1)
    n_tiles = seq_len // tm
    x = x_ref[...]
    d = x.shape[1]
    grp = d // len(POOL_WINDOWS)
    h = _rms_mod(x, g_ref[...], sc_ref[...], sh_ref[...])
    hp = _rms_mod(xp_ref[...], g_ref[...], sc_ref[...], sh_ref[...])
    hn = _rms_mod(xn_ref[...], g_ref[...], sc_ref[...], sh_ref[...])
    ext[0:HALO, :] = jnp.where(i > 0, hp, 0.0)
    ext[HALO:HALO + tm, :] = h
    ext[HALO + tm:2 * HALO + tm, :] = jnp.where(i < n_tiles - 1, hn, 0.0)
    pos = i * tm + lax.broadcasted_iota(jnp.int32, (tm, 1), 0)
    outs = []
    for gi, w in enumerate(POOL_WINDOWS):
        cols = slice(gi * grp, (gi + 1) * grp)
        s = ext[HALO - w // 2:HALO - w // 2 + tm, cols]
        for j in range(-w // 2 + 1, w // 2):
            s = s + ext[HALO + j:HALO + j + tm, cols]
        cnt = (jnp.minimum(pos + w // 2, seq_len) - jnp.maximum(pos - w // 2, 0)).astype(F32)
        outs.append(_bdot(s / cnt - h[:, cols], pw_ref[gi]))
    y = jnp.concatenate(outs, axis=1) * ps_ref[...]
    _ffn_pre(x + gt_ref[...] * y, gf_ref, scf_ref, shf_ref, rw_ref, x_out, hf_out, aff_out)


def _pool_layer(x, g_mix, sc_a, sh_a, pool_w, pool_scale, gt_a, g_ffn, sc_f, sh_f, router_w):
    b, t, d = x.shape
    ne = router_w.shape[1]
    ng, grp, _ = pool_w.shape
    assert max(POOL_WINDOWS) // 2 <= HALO
    tm = _tile(t, 256)
    hb = tm // HALO
    n_halo = t // HALO
    xt = pl.BlockSpec((None, tm, d), lambda bi, i: (bi, i, 0))
    vec = pl.BlockSpec((1, d), lambda bi, i: (0, 0))
    return pl.pallas_call(
        functools.partial(_pool_kernel, tm=tm, seq_len=t),
        grid=(b, t // tm),
        in_specs=[xt,
                  pl.BlockSpec((None, HALO, d), lambda bi, i: (bi, jnp.maximum(i * hb - 1, 0), 0)),
                  pl.BlockSpec((None, HALO, d), lambda bi, i: (bi, jnp.minimum((i + 1) * hb, n_halo - 1), 0)),
                  vec, _mod_specs(d), _mod_specs(d),
                  pl.BlockSpec((ng, grp, grp), lambda bi, i: (0, 0, 0)), vec, _mod_specs(d),
                  vec, _mod_specs(d), _mod_specs(d), pl.BlockSpec((d, ne), lambda bi, i: (0, 0))],
        out_specs=[xt, xt, pl.BlockSpec((None, tm, ne), lambda bi, i: (bi, i, 0))],
        out_shape=[jax.ShapeDtypeStruct((b, t, d), F32), jax.ShapeDtypeStruct((b, t, d), F32),
                   jax.ShapeDtypeStruct((b, t, ne), F32)],
        scratch_shapes=[pltpu.VMEM((tm + 2 * HALO, d), F32)],
        compiler_params=_cparams(("arbitrary", "arbitrary")),
        name="pool_ffnpre",
    )(x, x, x, g_mix.reshape(1, d), sc_a, sh_a, pool_w.astype(BF16), pool_scale.reshape(1, d), gt_a,
      g_ffn.reshape(1, d), sc_f, sh_f, router_w)


def _route_kernel(aff_ref, idx_ref, gate_ref, *, cap):
    nr = aff_ref.shape[0]
    aff = aff_ref[...]
    bits = pltpu.bitcast(aff, jnp.int32)
    capf = float(cap)

    def bit_step(i, thr):
        cand = thr | (jnp.int32(1) << (30 - i))
        cnt = jnp.sum(jnp.where(bits >= cand, 1.0, 0.0), keepdims=True)
        return jnp.where(cnt >= capf, cand, thr)

    thr = lax.fori_loop(0, 31, bit_step, jnp.zeros((1, 1), jnp.int32))
    gt = bits > thr
    eq = bits == thr
    need = capf - jnp.sum(jnp.where(gt, 1.0, 0.0), keepdims=True)

    ri = lax.broadcasted_iota(jnp.int32, (LANES, LANES), 0)
    ci = lax.broadcasted_iota(jnp.int32, (LANES, LANES), 1)
    upper = (ri <= ci).astype(BF16)
    rr = lax.broadcasted_iota(jnp.int32, (nr, nr), 0)
    rc = lax.broadcasted_iota(jnp.int32, (nr, nr), 1)
    lower_strict = (rc < rr).astype(BF16)
    ones8 = jnp.ones((8, LANES), BF16)
    p_iota = lax.broadcasted_iota(jnp.int32, (cap, nr), 0).astype(F32)
    r_iota = lax.broadcasted_iota(jnp.int32, (cap, nr), 1).astype(F32)
    c_iota = lax.broadcasted_iota(jnp.int32, (cap, LANES), 1).astype(F32)
    p_col = lax.broadcasted_iota(jnp.int32, (cap, 1), 0).astype(F32)
    upper_r = (rr <= rc).astype(BF16)

    eq_b = jnp.where(eq, 1.0, 0.0).astype(BF16)
    eq_rank = (jnp.dot(eq_b, upper, preferred_element_type=F32) - eq_b.astype(F32)
               + jnp.sum(jnp.dot(lower_strict, eq_b, preferred_element_type=F32), axis=1, keepdims=True))
    sel = gt | (eq & (eq_rank < need))
    sel_b = jnp.where(sel, 1.0, 0.0).astype(BF16)
    lc = jnp.dot(sel_b, upper, preferred_element_type=F32)
    row_tot = lax.dot_general(ones8, sel_b, (((1,), (1,)), ((), ())),
                              preferred_element_type=F32)[0:1, :]
    row_incl = jnp.dot(row_tot.astype(BF16), upper_r, preferred_element_type=F32)
    row_excl = row_incl - row_tot
    hit = (row_excl <= p_iota) & (p_iota < row_incl)
    hit_b = jnp.where(hit, 1.0, 0.0).astype(BF16)
    r_of_p = jnp.sum(jnp.where(hit, r_iota, 0.0), axis=1, keepdims=True)
    base = jnp.sum(jnp.where(hit, row_excl, 0.0), axis=1, keepdims=True)
    lc_row = jnp.dot(hit_b, lc.astype(BF16), preferred_element_type=F32)
    c_of_p = jnp.sum(jnp.where(lc_row <= (p_col - base), 1.0, 0.0), axis=1, keepdims=True)
    idx_ref[...] = (r_of_p * float(LANES) + c_of_p).astype(jnp.int32)
    a1 = aff.astype(BF16)
    a2 = (aff - a1.astype(F32)).astype(BF16)
    a3 = (aff - a1.astype(F32) - a2.astype(F32)).astype(BF16)
    a_row = (jnp.dot(hit_b, a1, preferred_element_type=F32) + jnp.dot(hit_b, a2, preferred_element_type=F32)
             + jnp.dot(hit_b, a3, preferred_element_type=F32))
    gate_ref[...] = jnp.sum(jnp.where(c_iota == c_of_p, a_row, 0.0), axis=1, keepdims=True)


def _route(aff):
    b, t, ne = aff.shape
    cap = max(1, CAPACITY_FACTOR * t // ne)
    assert t % LANES == 0
    nr = t // LANES
    aff_t = jnp.swapaxes(aff, 1, 2).reshape(b, ne, nr, LANES)
    out_blk = pl.BlockSpec((None, None, cap, 1), lambda bi, e: (bi, e, 0, 0))
    idx, gate = pl.pallas_call(
        functools.partial(_route_kernel, cap=cap),
        grid=(b, ne),
        in_specs=[pl.BlockSpec((None, None, nr, LANES), lambda bi, e: (bi, e, 0, 0))],
        out_specs=[out_blk, out_blk],
        out_shape=[jax.ShapeDtypeStruct((b, ne, cap, 1), jnp.int32),
                   jax.ShapeDtypeStruct((b, ne, cap, 1), F32)],
        compiler_params=_cparams(("arbitrary", "arbitrary")),
        name="expert_choice_route",
    )(aff_t)
    return idx.reshape(b, ne, cap), gate


def _moe_kernel(idx_ref, gate_ref, gt_ref, hf_hbm, x_hbm, wg_ref, wu_ref, wd_ref, xo_hbm,
                xs_buf, acc_buf, sem_xs, sem_acc, sem_out, *, tile, n_exp, n_tiles, n_steps):
    del x_hbm
    bi = pl.program_id(0)
    e = pl.program_id(1)
    ti = pl.program_id(2)
    step = (bi * n_exp + e) * n_tiles + ti
    slot = step % 2

    def gather_xs(step_, slot_):
        b_ = step_ // (n_exp * n_tiles)
        base_ = step_ * tile

        def body(k, carry):
            tok = idx_ref[base_ + k]
            pltpu.make_async_copy(hf_hbm.at[b_, pl.ds(tok, 1)], xs_buf.at[slot_, pl.ds(k, 1)],
                                  sem_xs.at[slot_]).start()
            return carry

        lax.fori_loop(0, tile, body, 0)

    def rows_acc(start):
        base = step * tile

        def body(k, carry):
            tok = idx_ref[base + k]
            if start == "gather":
                pltpu.make_async_copy(xo_hbm.at[bi, pl.ds(tok, 1)], acc_buf.at[pl.ds(k, 1)], sem_acc).start()
            else:
                pltpu.make_async_copy(acc_buf.at[pl.ds(k, 1)], xo_hbm.at[bi, pl.ds(tok, 1)], sem_out).start()
            return carry

        lax.fori_loop(0, tile, body, 0)

    @pl.when(step == 0)
    def _():
        gather_xs(step, slot)

    @pl.when(step > 0)
    def _():
        pltpu.make_async_copy(acc_buf, xo_hbm.at[0, pl.ds(0, tile)], sem_out).wait()

    rows_acc("gather")
    pltpu.make_async_copy(hf_hbm.at[0, pl.ds(0, tile)], xs_buf.at[slot], sem_xs.at[slot]).wait()

    @pl.when(step + 1 < n_steps)
    def _():
        gather_xs(step + 1, 1 - slot)

    h = xs_buf[slot].astype(BF16)
    g = jnp.dot(h, wg_ref[...], preferred_element_type=F32)
    u = jnp.dot(h, wu_ref[...], preferred_element_type=F32)
    hid = (g * jax.nn.sigmoid(g)) * u
    y = jnp.dot(hid.astype(BF16), wd_ref[...], preferred_element_type=F32)
    y = y * gate_ref[...] * gt_ref[...]
    pltpu.make_async_copy(xo_hbm.at[0, pl.ds(0, tile)], acc_buf, sem_acc).wait()
    acc_buf[...] = acc_buf[...] + y
    rows_acc("scatter")

    @pl.when(step == n_steps - 1)
    def _():
        pltpu.make_async_copy(acc_buf, xo_hbm.at[0, pl.ds(0, tile)], sem_out).wait()


def _moe(x, hf, idx, gate, gt_f, w_gate, w_up, w_down):
    b, t, d = x.shape
    ne, _, ff = w_gate.shape
    cap = idx.shape[2]
    tile = _tile(cap, 256)
    nt = cap // tile
    n_steps = b * ne * nt
    grid_spec = pltpu.PrefetchScalarGridSpec(
        num_scalar_prefetch=1,
        grid=(b, ne, nt),
        in_specs=[pl.BlockSpec((None, None, tile, 1), lambda bi, e, ti, idx_r: (bi, e, ti, 0)),
                  pl.BlockSpec((None, 1, d), lambda bi, e, ti, idx_r: (bi, 0, 0)),
                  pl.BlockSpec(memory_space=pl.ANY),
                  pl.BlockSpec(memory_space=pl.ANY),
                  pl.BlockSpec((None, d, ff), lambda bi, e, ti, idx_r: (e, 0, 0)),
                  pl.BlockSpec((None, d, ff), lambda bi, e, ti, idx_r: (e, 0, 0)),
                  pl.BlockSpec((None, ff, d), lambda bi, e, ti, idx_r: (e, 0, 0))],
        out_specs=pl.BlockSpec(memory_space=pl.ANY),
        scratch_shapes=[pltpu.VMEM((2, tile, d), F32), pltpu.VMEM((tile, d), F32),
                        pltpu.SemaphoreType.DMA((2,)), pltpu.SemaphoreType.DMA(()),
                        pltpu.SemaphoreType.DMA(())],
    )
    return pl.pallas_call(
        functools.partial(_moe_kernel, tile=tile, n_exp=ne, n_tiles=nt, n_steps=n_steps),
        grid_spec=grid_spec,
        out_shape=jax.ShapeDtypeStruct((b, t, d), F32),
        input_output_aliases={4: 0},
        compiler_params=_cparams(("arbitrary", "arbitrary", "arbitrary")),
        name="expert_ffn",
    )(idx.reshape(-1), gate, gt_f, hf, x, w_gate, w_up, w_down)


def _final_norm_kernel(x_ref, g_ref, o_ref):
    x = x_ref[...]
    o_ref[...] = x * lax.rsqrt(jnp.mean(x * x, axis=-1, keepdims=True) + NORM_EPS) * g_ref[...]


def _final_norm(x, g):
    b, t, d = x.shape
    tm = _tile(t, 512)
    xt = pl.BlockSpec((None, tm, d), lambda bi, i: (bi, i, 0))
    return pl.pallas_call(
        _final_norm_kernel,
        grid=(b, t // tm),
        in_specs=[xt, pl.BlockSpec((1, d), lambda bi, i: (0, 0))],
        out_specs=xt,
        out_shape=jax.ShapeDtypeStruct((b, t, d), F32),
        compiler_params=_cparams(("arbitrary", "arbitrary")),
        name="final_norm",
    )(x, g.reshape(1, d))


def _moe_layer(x, hf, aff, gt_f, w_gate, w_up, w_down):
    idx, gate = _route(aff)
    return _moe(x, hf, idx, gate, gt_f, w_gate.astype(BF16), w_up.astype(BF16), w_down.astype(BF16))


def kernel(x, c, ctx, c_ctx, ada_w, ada_b, norm_mix, norm_ffn, w_in, shift_mu, decay_w0, decay_w2, iclr_a0,
           iclr_a2, gate_g2, k_k, k_a, r_k, ln_w, ln_b, sink, w_out, pool_w, pool_scale, router_w, exp_w_gate,
           exp_w_up, exp_w_down, norm_final):
    b, t, d = x.shape
    depth = ada_w.shape[0]
    assert b + 1 <= 8
    cc = jnp.concatenate([c, c_ctx[None, :], jnp.zeros((8 - b - 1, d), F32)], axis=0)
    mod = _ada(cc, ada_w, ada_b)

    def mods(l, rows):
        return [mod[l, rows, None, i * d:(i + 1) * d] for i in range(6)]

    cos_t, sin_t = _rope_tables(t)
    for l in range(depth):
        sh_a, sc_a, gt_a, sh_f, sc_f, gt_f = mods(l, slice(0, b))
        if l % 2 == 0:
            e = l // 2
            csh_a, csc_a = [jnp.broadcast_to(m, (b, 1, d)) for m in mods(l, slice(b, b + 1))[:2]]
            w_p = _relayout_in_cols(w_in[e]).astype(BF16)
            mu_p = _relayout_in_cols(jnp.concatenate(
                [shift_mu[e], jnp.zeros((w_in.shape[2] - shift_mu.shape[1],), F32)]))
            mu_rkv = mu_p[None, :3 * RW]
            mu_lo = mu_p[None, COL_LORA:]
            px = _inproj(x, norm_mix[l], sc_a, sh_a, w_p, cos_t, sin_t)
            pc = _inproj(ctx, norm_mix[l], csc_a, csh_a, w_p)
            scan_args = (mu_rkv, mu_lo, decay_w0[e][:, None, :], decay_w2[e], iclr_a0[e][:, None, :], iclr_a2[e],
                         k_k[e][None], k_a[e][None], r_k[e][None])
            s0 = jnp.zeros((b, 2, HEADS, HEAD, HEAD), F32)
            s_ctx = _rwkv_scan(pc, s0, *scan_args)[4]
            yf, yb, bf, bb, _ = _rwkv_scan(px, s_ctx, *scan_args)
            g2_pad = jnp.zeros((LORA_W, RW), F32).at[LORA_GATE:LORA_GATE + GATE_LORA].set(gate_g2[e]).astype(BF16)
            rwkv_x = _rwkv_finish(yf, yb, bf, bb, px, mu_lo, g2_pad, ln_w[e][None], ln_b[e][None])
            att_x = _attention(px, pc, sink[e])
            x, hf, aff = _outproj(rwkv_x, att_x, w_out[e].astype(BF16), x, gt_a, norm_ffn[l], sc_f, sh_f,
                                  router_w[l].astype(BF16))
        else:
            o = l // 2
            x, hf, aff = _pool_layer(x, norm_mix[l], sc_a, sh_a, pool_w[o], pool_scale[o], gt_a, norm_ffn[l],
                                     sc_f, sh_f, router_w[l].astype(BF16))
        x = _moe_layer(x, hf, aff, gt_f, exp_w_gate[l], exp_w_up[l], exp_w_down[l])
    return _final_norm(x, norm_final)
```

```python
import functools

import jax
import jax.numpy as jnp
from jax import lax
from jax.experimental import pallas as pl
from jax.experimental.pallas import tpu as pltpu

F32 = jnp.float32
BF16 = jnp.bfloat16
HIGHEST = lax.Precision.HIGHEST

GRID_W = 64
HEADS = 16
HEAD = 64
RW = HEADS * HEAD
DECAY_LORA = 64
AAA_LORA = 64
GATE_LORA = 160
GN_EPS = 64e-5
KV_HEADS = 4
GROUP = HEADS // KV_HEADS
KVW = KV_HEADS * HEAD
ATT_SCALE = HEAD ** -0.5
WINDOW = 128
BLOCK = 128
ROPE_THETA = 10000.0
POOL_WINDOWS = (2, 4, 8, 16)
N_EXPERTS = 16
CAPACITY_FACTOR = 2
NORM_EPS = 1e-6

COL_RKV = 0
COL_Q = 3 * RW
COL_AK = 4 * RW
COL_AV = 4 * RW + KVW
COL_LORA = 4 * RW + 2 * KVW
LORA_W = 512
N_COLS = COL_LORA + LORA_W
LORA_GATE = 2 * DECAY_LORA + 2 * AAA_LORA
CHUNK = 64
QUAD = 4
QW = QUAD * HEAD
assert CHUNK == HEAD
HALO = 8
LANES = 128
VMEM_LIMIT = 56 * 1024 * 1024


def _cparams(sem):
    return pltpu.CompilerParams(dimension_semantics=sem, vmem_limit_bytes=VMEM_LIMIT)


def _tile(n, pref):
    t = min(n, pref)
    assert n % t == 0, (n, t)
    return t


def _bdot(a, b):
    return jnp.dot(a.astype(BF16), b.astype(BF16), preferred_element_type=F32)


def _bdot_nt(a, b):
    return lax.dot_general(a.astype(BF16), b.astype(BF16), (((1,), (1,)), ((), ())),
                           preferred_element_type=F32)


def _fdot(a, b):
    return jnp.dot(a, b, preferred_element_type=F32, precision=HIGHEST)


def _rms_mod(x, g, sc, sh):
    y = x * lax.rsqrt(jnp.mean(x * x, axis=-1, keepdims=True) + NORM_EPS)
    return (y * g) * (1.0 + sc) + sh


def _head_sum(x, bd):
    return jnp.concatenate(
        [_fdot(x[:, g * LANES:(g + 1) * LANES], bd) for g in range(x.shape[1] // LANES)], axis=1)


def _block_diag_ones():
    r = lax.broadcasted_iota(jnp.int32, (LANES, LANES), 0) // HEAD
    c = lax.broadcasted_iota(jnp.int32, (LANES, LANES), 1) // HEAD
    return (r == c).astype(F32)


def _ada_kernel(cc_ref, w_ref, b_ref, o_ref):
    cc = cc_ref[...]
    s = cc * jax.nn.sigmoid(cc)
    o_ref[...] = _bdot(s, w_ref[...]) + b_ref[...]


def _ada(cc, ada_w, ada_b):
    n_l, d, n = ada_w.shape
    tn = _tile(n, 1536)
    return pl.pallas_call(
        _ada_kernel,
        grid=(n_l, n // tn),
        in_specs=[pl.BlockSpec((8, d), lambda l, j: (0, 0)),
                  pl.BlockSpec((None, d, tn), lambda l, j: (l, 0, j)),
                  pl.BlockSpec((None, 1, tn), lambda l, j: (l, 0, j))],
        out_specs=pl.BlockSpec((None, 8, tn), lambda l, j: (l, 0, j)),
        out_shape=jax.ShapeDtypeStruct((n_l, 8, n), F32),
        compiler_params=_cparams(("arbitrary", "arbitrary")),
        name="adaln",
    )(cc, ada_w, ada_b.reshape(n_l, 1, n))


def _inproj_kernel(*refs, rope_lo, rope_hi, tn):
    if rope_hi > rope_lo:
        x_ref, g_ref, sc_ref, sh_ref, w_ref, cos_ref, sin_ref, o_ref, h_scr = refs
    else:
        x_ref, g_ref, sc_ref, sh_ref, w_ref, o_ref, h_scr = refs
    j = pl.program_id(2)

    @pl.when(j == 0)
    def _():
        h_scr[...] = _rms_mod(x_ref[...], g_ref[...], sc_ref[...], sh_ref[...]).astype(BF16)

    acc = jnp.dot(h_scr[...], w_ref[...], preferred_element_type=F32)
    if rope_hi > rope_lo:
        is_rope = (j >= rope_lo) & (j < rope_hi)

        @pl.when(is_rope)
        def _():
            cos = cos_ref[...]
            sin = sin_ref[...]
            lane = lax.broadcasted_iota(jnp.int32, cos.shape, 1)
            first = (lane % 32) < 16
            outs = []
            for g in range(tn // LANES):
                a = acc[:, g * LANES:(g + 1) * LANES]
                partner = jnp.where(first, pltpu.roll(a, LANES - 16, 1), pltpu.roll(a, 16, 1))
                outs.append(a * cos + partner * sin)
            o_ref[...] = jnp.concatenate(outs, axis=1)

        @pl.when(jnp.logical_not(is_rope))
        def _():
            o_ref[...] = acc
    else:
        o_ref[...] = acc


def _inproj(x, g, sc, sh, w, cos_t=None, sin_t=None):
    b, t, d = x.shape
    n = w.shape[1]
    tm = _tile(t, 1024)
    tn = 256
    rope = cos_t is not None
    rope_lo, rope_hi = (COL_Q // tn, COL_AV // tn) if rope else (0, 0)
    in_specs = [pl.BlockSpec((None, tm, d), lambda bi, i, j: (bi, i, 0)),
                pl.BlockSpec((1, d), lambda bi, i, j: (0, 0)),
                pl.BlockSpec((None, 1, d), lambda bi, i, j: (bi, 0, 0)),
                pl.BlockSpec((None, 1, d), lambda bi, i, j: (bi, 0, 0)),
                pl.BlockSpec((d, tn), lambda bi, i, j: (0, j))]
    args = [x, g.reshape(1, d), sc, sh, w]
    if rope:
        in_specs += [pl.BlockSpec((tm, LANES), lambda bi, i, j: (i, 0)),
                     pl.BlockSpec((tm, LANES), lambda bi, i, j: (i, 0))]
        args += [cos_t, sin_t]
    return pl.pallas_call(
        functools.partial(_inproj_kernel, rope_lo=rope_lo, rope_hi=rope_hi, tn=tn),
        grid=(b, t // tm, n // tn),
        in_specs=in_specs,
        out_specs=pl.BlockSpec((None, tm, tn), lambda bi, i, j: (bi, i, j)),
        out_shape=jax.ShapeDtypeStruct((b, t, n), F32),
        scratch_shapes=[pltpu.VMEM((tm, d), BF16)],
        compiler_params=_cparams(("arbitrary", "arbitrary", "arbitrary")),
        name="inproj_rope" if rope else "inproj_ctx",
    )(*args)


def _shift_lerp(p, prev_blk, next_blk, mu, has_prev, has_next):
    rows = p.shape[0]
    ridx = lax.broadcasted_iota(jnp.int32, p.shape, 0)
    prev_row = jnp.where(has_prev, prev_blk[HALO - 1:HALO, :], 0.0)
    next_row = jnp.where(has_next, next_blk[0:1, :], 0.0)
    prev = jnp.where(ridx == 0, prev_row, pltpu.roll(p, 1, 0))
    nxt = jnp.where(ridx == rows - 1, next_row, pltpu.roll(p, rows - 1, 0))
    return p + mu * (0.5 * (prev + nxt) - p)


def _rwkv_kernel(*refs, n_chunks, seq_len):
    (rkv_f, lo_f, rkv_fp, lo_fp, rkv_fn, lo_fn,
     rkv_b, lo_b, rkv_bp, lo_bp, rkv_bn, lo_bn,
     mu_rkv_ref, mu_lo_ref, w0_ref, w2_ref, a0_ref, a2_ref, kk_ref, ka_ref, rk_ref, s0_ref,
     yf_ref, yb_ref, bf_ref, bb_ref, sfin_ref, state) = refs
    c = pl.program_id(1)
    cn = CHUNK

    @pl.when(c == 0)
    def _():
        state[...] = s0_ref[...]

    bd = _block_diag_ones()
    ri = lax.broadcasted_iota(jnp.int32, (cn, cn), 0)
    ci = lax.broadcasted_iota(jnp.int32, (cn, cn), 1)
    qr = lax.broadcasted_iota(jnp.int32, (QW, QW), 0)
    qc = lax.broadcasted_iota(jnp.int32, (QW, QW), 1)
    same_head = (qr // HEAD) == (qc // HEAD)
    eye = (qr == qc).astype(F32)
    lane_head = lax.broadcasted_iota(jnp.int32, (cn, QW), 1) // HEAD
    mu_rkv = mu_rkv_ref[...]
    mu_lo = mu_lo_ref[...]
    k_k = kk_ref[...]
    k_a = ka_ref[...]
    r_k = rk_ref[...]

    dirs = ((0, c, rkv_f, lo_f, rkv_fp, lo_fp, rkv_fn, lo_fn, yf_ref, bf_ref),
            (1, n_chunks - 1 - c, rkv_b, lo_b, rkv_bp, lo_bp, rkv_bn, lo_bn, yb_ref, bb_ref))
    prep = []
    for d, cd, rkv_ref, lo_ref, rkv_p, lo_p, rkv_n, lo_n, y_ref, bon_ref in dirs:
        has_prev = cd > 0
        has_next = cd < n_chunks - 1
        m = _shift_lerp(rkv_ref[...], rkv_p[...], rkv_n[...], mu_rkv, has_prev, has_next)
        ml = _shift_lerp(lo_ref[...], lo_p[...], lo_n[...], mu_lo, has_prev, has_next)
        r = m[:, 0:RW]
        k = m[:, RW:2 * RW]
        v = m[:, 2 * RW:3 * RW]
        wd = ml[:, d * DECAY_LORA:(d + 1) * DECAY_LORA]
        ad = ml[:, 2 * DECAY_LORA + d * AAA_LORA:2 * DECAY_LORA + (d + 1) * AAA_LORA]
        w_pre = w0_ref[d] + _bdot(jnp.tanh(wd), w2_ref[d])
        z = -w_pre
        softplus = jnp.maximum(z, 0.0) + jnp.log(1.0 + jnp.exp(-jnp.abs(z)))
        logw = -jnp.exp(-softplus - 0.5)
        a = jax.nn.sigmoid(a0_ref[d] + _bdot(ad, a2_ref[d]))
        kk = k * k_k
        kk = kk * lax.rsqrt(jnp.maximum(_head_sum(kk * kk, bd), 1e-24))
        k_mod = k * (1.0 + (a - 1.0) * k_a)
        bon_ref[...] = _head_sum(r * k_mod * r_k, bd) * v

        if d == 0:
            tri = (ci <= ri).astype(F32)
            strict = (qc % cn) < (qr % cn)
            incl = (qc % cn) <= (qr % cn)
            last = cn - 1
        else:
            tri = (ci >= ri).astype(F32)
            strict = (qc % cn) > (qr % cn)
            incl = (qc % cn) >= (qr % cn)
            last = 0
        cum = _fdot(tri, logw)
        e_pos = jnp.exp(cum)
        e_neg = jnp.exp(-cum)
        p_last = e_pos[last:last + 1, :]
        bh = (kk * a) * e_neg
        kh = k_mod * e_neg
        prep.append(dict(d=d, at=(-kk) * jnp.exp(cum - logw), rt=r * e_pos, bh=bh, kh=kh, v=v,
                         bbar=bh * p_last, kbar=kh * p_last, p_last=p_last,
                         m_strict=same_head & strict, m_incl=same_head & incl, y_ref=y_ref))

    def rep(x):
        return jnp.concatenate([x] * QUAD, axis=0)

    def to_bd(x):
        return jnp.where(same_head, rep(x), 0.0)

    def collapse(x):
        out = jnp.where(lane_head == 0, x[0:cn], 0.0)
        for h in range(1, QUAD):
            out = out + jnp.where(lane_head == h, x[h * cn:(h + 1) * cn], 0.0)
        return out

    items = []
    for pr in prep:
        for q in range(HEADS // QUAD):
            sl = slice(q * QW, (q + 1) * QW)
            it = {name: pr[name][:, sl] for name in ("at", "rt", "bh", "kh", "v", "bbar", "kbar", "p_last")}
            it.update(d=pr["d"], q=q, m_strict=pr["m_strict"], m_incl=pr["m_incl"])
            items.append(it)

    for it in items:
        g = _bdot_nt(jnp.concatenate([to_bd(it["at"]), to_bd(it["rt"])], axis=0),
                     jnp.concatenate([rep(it["bh"]), rep(it["kh"])], axis=0))
        it["a_ak"] = jnp.where(it["m_strict"], g[:QW, QW:], 0.0)
        it["a_rb"] = jnp.where(it["m_incl"], g[QW:, :QW], 0.0)
        it["a_rk"] = jnp.where(it["m_incl"], g[QW:, QW:], 0.0)
        it["p"] = jnp.where(it["m_strict"], g[:QW, :QW], 0.0)
        it["t"] = eye + it["p"]
    for _ in range(5):
        for it in items:
            pb = it["p"].astype(BF16)
            it["p"] = jnp.dot(pb, pb, preferred_element_type=F32)
        for it in items:
            it["t"] = it["t"] + _bdot(it["t"], it["p"])
    for it in items:
        av = _bdot(jnp.concatenate([it["a_ak"], it["a_rk"]], axis=0), rep(it["v"]))
        it["aakv"] = collapse(av[:QW])
        it["y0"] = collapse(av[QW:])
    for it in items:
        w = _bdot(it["t"], jnp.concatenate([rep(it["at"]), rep(it["aakv"])], axis=1))
        it["w1"] = collapse(w[:, :QW])
        it["w2"] = collapse(w[:, QW:])
        it["xt"] = jnp.concatenate([it["bbar"], it["kbar"]], axis=0).T
        it["pcol"] = jnp.broadcast_to(it["p_last"], (LANES, QW)).T[:, 0:1]
    for it in items:
        it["s"] = state[it["d"], it["q"]]
    for it in items:
        us0 = _bdot(jnp.concatenate([it["w1"], it["rt"]], axis=0), it["s"])
        it["u"] = us0[:cn] + it["w2"]
        it["ys"] = us0[cn:]
    for it in items:
        it["y"] = it["ys"] + collapse(_bdot(it["a_rb"], rep(it["u"]))) + it["y0"]
    for it in items:
        upd = _bdot(it["xt"], jnp.concatenate([it["u"], it["v"]], axis=0))
        state[it["d"], it["q"]] = it["s"] * it["pcol"] + jnp.where(same_head, upd, 0.0)
    for pr in prep:
        pr["y_ref"][...] = jnp.concatenate([it["y"] for it in items if it["d"] == pr["d"]], axis=1)

    @pl.when(c == n_chunks - 1)
    def _():
        sfin_ref[...] = state[...]


def _rwkv_scan(px, s0, mu_rkv, mu_lo, w0, w2, a0, a2, k_k, k_a, r_k):
    b, t, _ = px.shape
    cn = CHUNK
    nc = t // cn
    assert t % cn == 0 and cn % HALO == 0
    hb = cn // HALO
    n_halo = t // HALO
    lora_blk = COL_LORA // LORA_W

    def fwd(ci):
        return ci

    def bwd(ci):
        return nc - 1 - ci

    def specs(cmap):
        return [
            pl.BlockSpec((None, cn, 3 * RW), lambda bi, ci: (bi, cmap(ci), 0)),
            pl.BlockSpec((None, cn, LORA_W), lambda bi, ci: (bi, cmap(ci), lora_blk)),
            pl.BlockSpec((None, HALO, 3 * RW), lambda bi, ci: (bi, jnp.maximum(cmap(ci) * hb - 1, 0), 0)),
            pl.BlockSpec((None, HALO, LORA_W),
                         lambda bi, ci: (bi, jnp.maximum(cmap(ci) * hb - 1, 0), lora_blk)),
            pl.BlockSpec((None, HALO, 3 * RW),
                         lambda bi, ci: (bi, jnp.minimum((cmap(ci) + 1) * hb, n_halo - 1), 0)),
            pl.BlockSpec((None, HALO, LORA_W),
                         lambda bi, ci: (bi, jnp.minimum((cmap(ci) + 1) * hb, n_halo - 1), lora_blk)),
        ]

    def const(shape):
        nd = len(shape)
        return pl.BlockSpec(shape, lambda bi, ci: (0,) * nd)

    in_specs = specs(fwd) + specs(bwd) + [
        const((1, 3 * RW)), const((1, LORA_W)),
        const((2, 1, RW)), const((2, DECAY_LORA, RW)), const((2, 1, RW)), const((2, AAA_LORA, RW)),
        const((1, RW)), const((1, RW)), const((1, RW)),
        pl.BlockSpec((None, 2, HEADS // QUAD, QW, QW), lambda bi, ci: (bi, 0, 0, 0, 0)),
    ]
    yshape = jax.ShapeDtypeStruct((b, t, RW), F32)
    out_specs = [
        pl.BlockSpec((None, cn, RW), lambda bi, ci: (bi, ci, 0)),
        pl.BlockSpec((None, cn, RW), lambda bi, ci: (bi, nc - 1 - ci, 0)),
        pl.BlockSpec((None, cn, RW), lambda bi, ci: (bi, ci, 0)),
        pl.BlockSpec((None, cn, RW), lambda bi, ci: (bi, nc - 1 - ci, 0)),
        pl.BlockSpec((None, 2, HEADS // QUAD, QW, QW), lambda bi, ci: (bi, 0, 0, 0, 0)),
    ]
    px_args = [px] * 12
    return pl.pallas_call(
        functools.partial(_rwkv_kernel, n_chunks=nc, seq_len=t),
        grid=(b, nc),
        in_specs=in_specs,
        out_specs=out_specs,
        out_shape=[yshape, yshape, yshape, yshape,
                   jax.ShapeDtypeStruct((b, 2, HEADS // QUAD, QW, QW), F32)],
        scratch_shapes=[pltpu.VMEM((2, HEADS // QUAD, QW, QW), F32)],
        compiler_params=_cparams(("arbitrary", "arbitrary")),
        name="rwkv_scan",
    )(*px_args, mu_rkv, mu_lo, w0, w2.astype(BF16), a0, a2.astype(BF16), k_k, k_a, r_k, s0)


def _relayout_in_cols(a):
    n_lora = 2 * DECAY_LORA + 2 * AAA_LORA + GATE_LORA
    rkv = a[..., :3 * RW]
    lora = a[..., 3 * RW:3 * RW + n_lora]
    att = a[..., 3 * RW + n_lora:]
    pad = jnp.zeros(a.shape[:-1] + (LORA_W - n_lora,), a.dtype)
    return jnp.concatenate([rkv, att, lora, pad], axis=-1)


def _rope_tables(t):
    rows = t // GRID_W
    row = jnp.repeat(jnp.arange(rows, dtype=F32), GRID_W)
    col = jnp.tile(jnp.arange(GRID_W, dtype=F32), rows)
    n_freq = HEAD // 4
    inv = ROPE_THETA ** (-jnp.arange(n_freq, dtype=F32) / n_freq)
    ar = row[:, None] * inv
    ac = col[:, None] * inv
    cos = jnp.concatenate([jnp.cos(ar), jnp.cos(ar), jnp.cos(ac), jnp.cos(ac)], axis=-1)
    sin = jnp.concatenate([-jnp.sin(ar), jnp.sin(ar), -jnp.sin(ac), jnp.sin(ac)], axis=-1)
    reps = LANES // HEAD
    return jnp.tile(cos, (1, reps)), jnp.tile(sin, (1, reps))


def _finish_kernel(yf_ref, yb_ref, bf_ref, bb_ref, lo_ref, lo_p, lo_n, mu_lo_ref, g2_ref, lnw_ref, lnb_ref,
                   o_ref, *, n_tiles):
    i = pl.program_id(1)
    bd = _block_diag_ones()
    y = yf_ref[...] + yb_ref[...]
    mean = _head_sum(y, bd) * (1.0 / HEAD)
    yc = y - mean
    var = _head_sum(yc * yc, bd) * (1.0 / HEAD)
    yn = yc * lax.rsqrt(var + GN_EPS) * lnw_ref[...] + lnb_ref[...]
    ml = _shift_lerp(lo_ref[...], lo_p[...], lo_n[...], mu_lo_ref[...], i > 0, i < n_tiles - 1)
    gate = _bdot(jax.nn.sigmoid(ml), g2_ref[...])
    o_ref[...] = ((yn + bf_ref[...] + bb_ref[...]) * gate).astype(o_ref.dtype)


def _rwkv_finish(yf, yb, bf, bb, px, mu_lo, g2_pad, ln_w, ln_b):
    b, t, _ = yf.shape
    tm = _tile(t, 256)
    nt = t // tm
    hb = tm // HALO
    n_halo = t // HALO
    lora_blk = COL_LORA // LORA_W
    tile = pl.BlockSpec((None, tm, RW), lambda bi, i: (bi, i, 0))
    vec = pl.BlockSpec((1, RW), lambda bi, i: (0, 0))
    return pl.pallas_call(
        functools.partial(_finish_kernel, n_tiles=nt),
        grid=(b, nt),
        in_specs=[tile, tile, tile, tile,
                  pl.BlockSpec((None, tm, LORA_W), lambda bi, i: (bi, i, lora_blk)),
                  pl.BlockSpec((None, HALO, LORA_W), lambda bi, i: (bi, jnp.maximum(i * hb - 1, 0), lora_blk)),
                  pl.BlockSpec((None, HALO, LORA_W),
                               lambda bi, i: (bi, jnp.minimum((i + 1) * hb, n_halo - 1), lora_blk)),
                  pl.BlockSpec((1, LORA_W), lambda bi, i: (0, 0)),
                  pl.BlockSpec((LORA_W, RW), lambda bi, i: (0, 0)),
                  vec, vec],
        out_specs=tile,
        out_shape=jax.ShapeDtypeStruct((b, t, RW), BF16),
        compiler_params=_cparams(("arbitrary", "arbitrary")),
        name="rwkv_finish",
    )(yf, yb, bf, bb, px, px, px, mu_lo, g2_pad, ln_w, ln_b)


def _attn_kernel(q_ref, kp_ref, kc_ref, kn_ref, vp_ref, vc_ref, vn_ref, ck_ref, cv_ref, sink_ref, o_ref,
                 *, n_blocks, n_ctx):
    n = pl.program_id(1)
    q = q_ref[...]
    k_win = jnp.concatenate([kp_ref[...], kc_ref[...], kn_ref[...]], axis=0)
    v_win = jnp.concatenate([vp_ref[...], vc_ref[...], vn_ref[...]], axis=0)
    k_all = jnp.concatenate([ck_ref[...], k_win], axis=0).astype(BF16)
    v_all = jnp.concatenate([cv_ref[...], v_win], axis=0).astype(BF16)
    rows = GROUP * BLOCK
    span = n_ctx + 3 * BLOCK
    qi = lax.broadcasted_iota(jnp.int32, (rows, span), 0) % BLOCK + BLOCK
    kj = lax.broadcasted_iota(jnp.int32, (rows, span), 1) - n_ctx
    k_lo = jnp.where(n == 0, BLOCK, 0)
    k_hi = jnp.where(n == n_blocks - 1, 2 * BLOCK, 3 * BLOCK)
    in_win = (jnp.abs(qi - kj) <= WINDOW) & (kj >= k_lo) & (kj < k_hi)
    ok = (kj < 0) | in_win
    sink = sink_ref[...]
    outs = []
    for g in range(KV_HEADS):
        qg = jnp.concatenate([q[:, (g * GROUP + i) * HEAD:(g * GROUP + i + 1) * HEAD] for i in range(GROUP)],
                             axis=0)
        kg = k_all[:, g * HEAD:(g + 1) * HEAD]
        vg = v_all[:, g * HEAD:(g + 1) * HEAD]
        s = _bdot_nt(qg, kg) * ATT_SCALE
        s = jnp.where(ok, s, -jnp.inf)
        sk = jnp.concatenate([jnp.broadcast_to(sink[:, g * GROUP + i:g * GROUP + i + 1], (BLOCK, 1))
                              for i in range(GROUP)], axis=0)
        m = jnp.maximum(jnp.max(s, axis=-1, keepdims=True), sk)
        p = jnp.exp(s - m)
        denom = jnp.sum(p, axis=-1, keepdims=True) + jnp.exp(sk - m)
        og = _bdot(p, vg) / denom
        outs += [og[i * BLOCK:(i + 1) * BLOCK] for i in range(GROUP)]
    o_ref[...] = jnp.concatenate(outs, axis=1).astype(o_ref.dtype)


def _attention(px, pc, sink):
    b, t, _ = px.shape
    n_ctx = pc.shape[1]
    nb = t // BLOCK
    kb = COL_AK // KVW
    vb = COL_AV // KVW

    def blk(col, off):
        return pl.BlockSpec((None, BLOCK, KVW), lambda bi, n: (bi, jnp.clip(n + off, 0, nb - 1), col))

    return pl.pallas_call(
        functools.partial(_attn_kernel, n_blocks=nb, n_ctx=n_ctx),
        grid=(b, nb),
        in_specs=[pl.BlockSpec((None, BLOCK, RW), lambda bi, n: (bi, n, COL_Q // RW)),
                  blk(kb, -1), blk(kb, 0), blk(kb, 1), blk(vb, -1), blk(vb, 0), blk(vb, 1),
                  pl.BlockSpec((None, n_ctx, KVW), lambda bi, n: (bi, 0, kb)),
                  pl.BlockSpec((None, n_ctx, KVW), lambda bi, n: (bi, 0, vb)),
                  pl.BlockSpec((1, HEADS), lambda bi, n: (0, 0))],
        out_specs=pl.BlockSpec((None, BLOCK, RW), lambda bi, n: (bi, n, 0)),
        out_shape=jax.ShapeDtypeStruct((b, t, RW), BF16),
        compiler_params=_cparams(("arbitrary", "arbitrary")),
        name="window_attention",
    )(px, px, px, px, px, px, px, pc, pc, sink.reshape(1, HEADS))


def _ffn_pre(x_new, gf_ref, scf_ref, shf_ref, rw_ref, xh_out, aff_out):
    d = x_new.shape[1]
    xh_out[:, 0:d] = x_new
    hf = _rms_mod(x_new, gf_ref[...], scf_ref[...], shf_ref[...])
    xh_out[:, d:2 * d] = hf
    logits = _bdot(hf, rw_ref[...])
    e = jnp.exp(logits - jnp.max(logits, axis=-1, keepdims=True))
    aff_out[...] = e / jnp.sum(e, axis=-1, keepdims=True)


def _outproj_kernel(ra_ref, at_ref, w_ref, x_ref, gt_ref, gf_ref, scf_ref, shf_ref, rw_ref,
                    xh_out, aff_out):
    y = (jnp.dot(ra_ref[...], w_ref[0:RW, :], preferred_element_type=F32)
         + jnp.dot(at_ref[...], w_ref[RW:2 * RW, :], preferred_element_type=F32))
    _ffn_pre(x_ref[...] + gt_ref[...] * y, gf_ref, scf_ref, shf_ref, rw_ref, xh_out, aff_out)


def _mod_specs(d):
    return pl.BlockSpec((None, 1, d), lambda bi, i: (bi, 0, 0))


def _outproj(rwkv_x, att_x, w_out, x, gt_a, g_ffn, sc_f, sh_f, router_w):
    b, t, d = x.shape
    ne = router_w.shape[1]
    tm = _tile(t, 256)
    xt = pl.BlockSpec((None, tm, d), lambda bi, i: (bi, i, 0))
    mt = pl.BlockSpec((None, tm, RW), lambda bi, i: (bi, i, 0))
    return pl.pallas_call(
        _outproj_kernel,
        grid=(b, t // tm),
        in_specs=[mt, mt, pl.BlockSpec((2 * RW, d), lambda bi, i: (0, 0)), xt, _mod_specs(d),
                  pl.BlockSpec((1, d), lambda bi, i: (0, 0)), _mod_specs(d), _mod_specs(d),
                  pl.BlockSpec((d, ne), lambda bi, i: (0, 0))],
        out_specs=[pl.BlockSpec((None, tm, 2 * d), lambda bi, i: (bi, i, 0)),
                   pl.BlockSpec((None, tm, ne), lambda bi, i: (bi, i, 0))],
        out_shape=[jax.ShapeDtypeStruct((b, t, 2 * d), F32), jax.ShapeDtypeStruct((b, t, ne), F32)],
        compiler_params=_cparams(("arbitrary", "arbitrary")),
        name="outproj_ffnpre",
    )(rwkv_x, att_x, w_out, x, gt_a, g_ffn.reshape(1, d), sc_f, sh_f, router_w)


def _pool_kernel(x_ref, xp_ref, xn_ref, g_ref, sc_ref, sh_ref, pw_ref, ps_ref, gt_ref,
                 gf_ref, scf_ref, shf_ref, rw_ref, xh_out, aff_out, ext, *, tm, seq_len):
    i = pl.program_id(1)
    n_tiles = seq_len // tm
    x = x_ref[...]
    d = x.shape[1]
    grp = d // len(POOL_WINDOWS)
    h = _rms_mod(x, g_ref[...], sc_ref[...], sh_ref[...])
    hp = _rms_mod(xp_ref[...], g_ref[...], sc_ref[...], sh_ref[...])
    hn = _rms_mod(xn_ref[...], g_ref[...], sc_ref[...], sh_ref[...])
    ext[0:HALO, :] = jnp.where(i > 0, hp, 0.0)
    ext[HALO:HALO + tm, :] = h
    ext[HALO + tm:2 * HALO + tm, :] = jnp.where(i < n_tiles - 1, hn, 0.0)
    pos = i * tm + lax.broadcasted_iota(jnp.int32, (tm, 1), 0)
    outs = []
    for gi, w in enumerate(POOL_WINDOWS):
        cols = slice(gi * grp, (gi + 1) * grp)
        s = ext[HALO - w // 2:HALO - w // 2 + tm, cols]
        for j in range(-w // 2 + 1, w // 2):
            s = s + ext[HALO + j:HALO + j + tm, cols]
        cnt = (jnp.minimum(pos + w // 2, seq_len) - jnp.maximum(pos - w // 2, 0)).astype(F32)
        outs.append(_bdot(s / cnt - h[:, cols], pw_ref[gi]))
    y = jnp.concatenate(outs, axis=1) * ps_ref[...]
    _ffn_pre(x + gt_ref[...] * y, gf_ref, scf_ref, shf_ref, rw_ref, xh_out, aff_out)


def _pool_layer(x, g_mix, sc_a, sh_a, pool_w, pool_scale, gt_a, g_ffn, sc_f, sh_f, router_w):
    b, t, _ = x.shape
    d = g_mix.shape[0]
    ne = router_w.shape[1]
    ng, grp, _ = pool_w.shape
    assert max(POOL_WINDOWS) // 2 <= HALO
    tm = _tile(t, 256)
    hb = tm // HALO
    n_halo = t // HALO
    xt = pl.BlockSpec((None, tm, d), lambda bi, i: (bi, i, 0))
    vec = pl.BlockSpec((1, d), lambda bi, i: (0, 0))
    return pl.pallas_call(
        functools.partial(_pool_kernel, tm=tm, seq_len=t),
        grid=(b, t // tm),
        in_specs=[xt,
                  pl.BlockSpec((None, HALO, d), lambda bi, i: (bi, jnp.maximum(i * hb - 1, 0), 0)),
                  pl.BlockSpec((None, HALO, d), lambda bi, i: (bi, jnp.minimum((i + 1) * hb, n_halo - 1), 0)),
                  vec, _mod_specs(d), _mod_specs(d),
                  pl.BlockSpec((ng, grp, grp), lambda bi, i: (0, 0, 0)), vec, _mod_specs(d),
                  vec, _mod_specs(d), _mod_specs(d), pl.BlockSpec((d, ne), lambda bi, i: (0, 0))],
        out_specs=[pl.BlockSpec((None, tm, 2 * d), lambda bi, i: (bi, i, 0)),
                   pl.BlockSpec((None, tm, ne), lambda bi, i: (bi, i, 0))],
        out_shape=[jax.ShapeDtypeStruct((b, t, 2 * d), F32), jax.ShapeDtypeStruct((b, t, ne), F32)],
        scratch_shapes=[pltpu.VMEM((tm + 2 * HALO, d), F32)],
        compiler_params=_cparams(("arbitrary", "arbitrary")),
        name="pool_ffnpre",
    )(x, x, x, g_mix.reshape(1, d), sc_a, sh_a, pool_w.astype(BF16), pool_scale.reshape(1, d), gt_a,
      g_ffn.reshape(1, d), sc_f, sh_f, router_w)


def _route_kernel(aff_ref, idx_ref, gate_ref, *, cap):
    nr = aff_ref.shape[0]
    aff = aff_ref[...]
    bits = pltpu.bitcast(aff, jnp.int32)
    capf = float(cap)

    def bit_step(i, thr):
        cand = thr | (jnp.int32(1) << (30 - i))
        cnt = jnp.sum(jnp.where(bits >= cand, 1.0, 0.0), keepdims=True)
        return jnp.where(cnt >= capf, cand, thr)

    thr = lax.fori_loop(0, 31, bit_step, jnp.zeros((1, 1), jnp.int32))
    gt = bits > thr
    eq = bits == thr
    need = capf - jnp.sum(jnp.where(gt, 1.0, 0.0), keepdims=True)

    ri = lax.broadcasted_iota(jnp.int32, (LANES, LANES), 0)
    ci = lax.broadcasted_iota(jnp.int32, (LANES, LANES), 1)
    upper = (ri <= ci).astype(BF16)
    rr = lax.broadcasted_iota(jnp.int32, (nr, nr), 0)
    rc = lax.broadcasted_iota(jnp.int32, (nr, nr), 1)
    lower_strict = (rc < rr).astype(BF16)
    ones8 = jnp.ones((8, LANES), BF16)
    p_iota = lax.broadcasted_iota(jnp.int32, (cap, nr), 0).astype(F32)
    r_iota = lax.broadcasted_iota(jnp.int32, (cap, nr), 1).astype(F32)
    c_iota = lax.broadcasted_iota(jnp.int32, (cap, LANES), 1).astype(F32)
    p_col = lax.broadcasted_iota(jnp.int32, (cap, 1), 0).astype(F32)
    upper_r = (rr <= rc).astype(BF16)

    eq_b = jnp.where(eq, 1.0, 0.0).astype(BF16)
    eq_rank = (jnp.dot(eq_b, upper, preferred_element_type=F32) - eq_b.astype(F32)
               + jnp.sum(jnp.dot(lower_strict, eq_b, preferred_element_type=F32), axis=1, keepdims=True))
    sel = gt | (eq & (eq_rank < need))
    sel_b = jnp.where(sel, 1.0, 0.0).astype(BF16)
    lc = jnp.dot(sel_b, upper, preferred_element_type=F32)
    row_tot = lax.dot_general(ones8, sel_b, (((1,), (1,)), ((), ())),
                              preferred_element_type=F32)[0:1, :]
    row_incl = jnp.dot(row_tot.astype(BF16), upper_r, preferred_element_type=F32)
    row_excl = row_incl - row_tot
    hit = (row_excl <= p_iota) & (p_iota < row_incl)
    hit_b = jnp.where(hit, 1.0, 0.0).astype(BF16)
    r_of_p = jnp.sum(jnp.where(hit, r_iota, 0.0), axis=1, keepdims=True)
    base = jnp.sum(jnp.where(hit, row_excl, 0.0), axis=1, keepdims=True)
    lc_row = jnp.dot(hit_b, lc.astype(BF16), preferred_element_type=F32)
    c_of_p = jnp.sum(jnp.where(lc_row <= (p_col - base), 1.0, 0.0), axis=1, keepdims=True)
    idx_ref[...] = (r_of_p * float(LANES) + c_of_p).astype(jnp.int32)
    a1 = aff.astype(BF16)
    a2 = (aff - a1.astype(F32)).astype(BF16)
    a3 = (aff - a1.astype(F32) - a2.astype(F32)).astype(BF16)
    a_row = (jnp.dot(hit_b, a1, preferred_element_type=F32) + jnp.dot(hit_b, a2, preferred_element_type=F32)
             + jnp.dot(hit_b, a3, preferred_element_type=F32))
    gate_ref[...] = jnp.sum(jnp.where(c_iota == c_of_p, a_row, 0.0), axis=1, keepdims=True)


def _route(aff):
    b, t, ne = aff.shape
    cap = max(1, CAPACITY_FACTOR * t // ne)
    assert t % LANES == 0
    nr = t // LANES
    aff_t = jnp.swapaxes(aff, 1, 2).reshape(b, ne, nr, LANES)
    out_blk = pl.BlockSpec((None, None, cap, 1), lambda bi, e: (bi, e, 0, 0))
    idx, gate = pl.pallas_call(
        functools.partial(_route_kernel, cap=cap),
        grid=(b, ne),
        in_specs=[pl.BlockSpec((None, None, nr, LANES), lambda bi, e: (bi, e, 0, 0))],
        out_specs=[out_blk, out_blk],
        out_shape=[jax.ShapeDtypeStruct((b, ne, cap, 1), jnp.int32),
                   jax.ShapeDtypeStruct((b, ne, cap, 1), F32)],
        compiler_params=_cparams(("arbitrary", "arbitrary")),
        name="expert_choice_route",
    )(aff_t)
    return idx.reshape(b, ne, cap), gate


def _moe_kernel(idx_ref, gate_ref, gt_ref, xh_in, wg_ref, wu_ref, wd_ref, xh_hbm,
                buf, obuf, gsem, osem, *, tile, n_exp, n_tiles, n_steps, d):
    del xh_in
    bi = pl.program_id(0)
    e = pl.program_id(1)
    ti = pl.program_id(2)
    step = (bi * n_exp + e) * n_tiles + ti
    slot = step % 2

    def row_in(tok, k, b_, slot_):
        return pltpu.make_async_copy(xh_hbm.at[b_, pl.ds(tok, 1)], buf.at[slot_, pl.ds(k, 1)], gsem.at[slot_])

    def wait_rows_in(slot_):
        pltpu.make_async_copy(xh_hbm.at[0, pl.ds(0, tile)], buf.at[slot_], gsem.at[slot_]).wait()

    def wait_rows_out():
        pltpu.make_async_copy(obuf, xh_hbm.at[0, pl.ds(0, tile), pl.ds(0, d)], osem).wait()

    def fetch_now():
        def body(k, carry):
            row_in(idx_ref[step * tile + k], k, bi, slot).start()
            return carry

        lax.fori_loop(0, tile, body, 0, unroll=8)

    starts_expert = (ti == 0) & (e > 0)

    @pl.when(starts_expert)
    def _():
        wait_rows_out()
        fetch_now()

    @pl.when(step == 0)
    def _():
        fetch_now()

    wait_rows_in(slot)

    placeholder = ((ti == n_tiles - 1) & (e < n_exp - 1)) | (step == n_steps - 1)
    src_step = jnp.where(placeholder, step, step + 1)
    src_b = src_step // (n_exp * n_tiles)
    for k in range(tile):
        row_in(idx_ref[src_step * tile + k], k, src_b, 1 - slot).start()

    xh = buf[slot]
    h = xh[:, d:].astype(BF16)
    g = jnp.dot(h, wg_ref[...], preferred_element_type=F32)
    u = jnp.dot(h, wu_ref[...], preferred_element_type=F32)
    hid = (g * jax.nn.sigmoid(g)) * u
    y = jnp.dot(hid.astype(BF16), wd_ref[...], preferred_element_type=F32)
    new_rows = xh[:, :d] + y * gate_ref[...] * gt_ref[...]

    @pl.when((step > 0) & jnp.logical_not(starts_expert))
    def _():
        wait_rows_out()

    @pl.when(placeholder)
    def _():
        wait_rows_in(1 - slot)

    obuf[...] = new_rows

    def put(k, carry):
        tok = idx_ref[step * tile + k]
        pltpu.make_async_copy(obuf.at[pl.ds(k, 1)], xh_hbm.at[bi, pl.ds(tok, 1), pl.ds(0, d)], osem).start()
        return carry

    lax.fori_loop(0, tile, put, 0, unroll=8)

    @pl.when(step == n_steps - 1)
    def _():
        wait_rows_out()


def _moe(xh, idx, gate, gt_f, w_gate, w_up, w_down):
    b, t, d2 = xh.shape
    d = d2 // 2
    ne, _, ff = w_gate.shape
    cap = idx.shape[2]
    tile = _tile(cap, 256)
    nt = cap // tile
    n_steps = b * ne * nt
    grid_spec = pltpu.PrefetchScalarGridSpec(
        num_scalar_prefetch=1,
        grid=(b, ne, nt),
        in_specs=[pl.BlockSpec((None, None, tile, 1), lambda bi, e, ti, idx_r: (bi, e, ti, 0)),
                  pl.BlockSpec((None, 1, d), lambda bi, e, ti, idx_r: (bi, 0, 0)),
                  pl.BlockSpec(memory_space=pl.ANY),
                  pl.BlockSpec((None, d, ff), lambda bi, e, ti, idx_r: (e, 0, 0)),
                  pl.BlockSpec((None, d, ff), lambda bi, e, ti, idx_r: (e, 0, 0)),
                  pl.BlockSpec((None, ff, d), lambda bi, e, ti, idx_r: (e, 0, 0))],
        out_specs=pl.BlockSpec(memory_space=pl.ANY),
        scratch_shapes=[pltpu.VMEM((2, tile, d2), F32), pltpu.VMEM((tile, d), F32),
                        pltpu.SemaphoreType.DMA((2,)), pltpu.SemaphoreType.DMA(())],
    )
    return pl.pallas_call(
        functools.partial(_moe_kernel, tile=tile, n_exp=ne, n_tiles=nt, n_steps=n_steps, d=d),
        grid_spec=grid_spec,
        out_shape=jax.ShapeDtypeStruct((b, t, d2), F32),
        input_output_aliases={3: 0},
        compiler_params=_cparams(("arbitrary", "arbitrary", "arbitrary")),
        name="expert_ffn",
    )(idx.reshape(-1), gate, gt_f, xh, w_gate, w_up, w_down)


def _final_norm_kernel(x_ref, g_ref, o_ref):
    x = x_ref[...]
    o_ref[...] = x * lax.rsqrt(jnp.mean(x * x, axis=-1, keepdims=True) + NORM_EPS) * g_ref[...]


def _final_norm(x, g):
    b, t, _ = x.shape
    d = g.shape[0]
    tm = _tile(t, 512)
    xt = pl.BlockSpec((None, tm, d), lambda bi, i: (bi, i, 0))
    return pl.pallas_call(
        _final_norm_kernel,
        grid=(b, t // tm),
        in_specs=[xt, pl.BlockSpec((1, d), lambda bi, i: (0, 0))],
        out_specs=xt,
        out_shape=jax.ShapeDtypeStruct((b, t, d), F32),
        compiler_params=_cparams(("arbitrary", "arbitrary")),
        name="final_norm",
    )(x, g.reshape(1, d))


def _moe_layer(xh, aff, gt_f, w_gate, w_up, w_down):
    idx, gate = _route(aff)
    return _moe(xh, idx, gate, gt_f, w_gate.astype(BF16), w_up.astype(BF16), w_down.astype(BF16))


def kernel(x, c, ctx, c_ctx, ada_w, ada_b, norm_mix, norm_ffn, w_in, shift_mu, decay_w0, decay_w2, iclr_a0,
           iclr_a2, gate_g2, k_k, k_a, r_k, ln_w, ln_b, sink, w_out, pool_w, pool_scale, router_w, exp_w_gate,
           exp_w_up, exp_w_down, norm_final):
    b, t, d = x.shape
    depth = ada_w.shape[0]
    assert depth == 2, "the context stream is only advanced for deeper stacks; not implemented"
    assert b + 1 <= 8
    cc = jnp.concatenate([c, c_ctx[None, :], jnp.zeros((8 - b - 1, d), F32)], axis=0)
    mod = _ada(cc, ada_w, ada_b)

    def mods(l, rows):
        return [mod[l, rows, None, i * d:(i + 1) * d] for i in range(6)]

    cos_t, sin_t = _rope_tables(t)
    for l in range(depth):
        sh_a, sc_a, gt_a, sh_f, sc_f, gt_f = mods(l, slice(0, b))
        if l % 2 == 0:
            e = l // 2
            csh_a, csc_a = [jnp.broadcast_to(m, (b, 1, d)) for m in mods(l, slice(b, b + 1))[:2]]
            w_p = _relayout_in_cols(w_in[e]).astype(BF16)
            mu_p = _relayout_in_cols(jnp.concatenate(
                [shift_mu[e], jnp.zeros((w_in.shape[2] - shift_mu.shape[1],), F32)]))
            mu_rkv = mu_p[None, :3 * RW]
            mu_lo = mu_p[None, COL_LORA:]
            px = _inproj(x, norm_mix[l], sc_a, sh_a, w_p, cos_t, sin_t)
            pc = _inproj(ctx, norm_mix[l], csc_a, csh_a, w_p)
            scan_args = (mu_rkv, mu_lo, decay_w0[e][:, None, :], decay_w2[e], iclr_a0[e][:, None, :], iclr_a2[e],
                         k_k[e][None], k_a[e][None], r_k[e][None])
            s0 = jnp.zeros((b, 2, HEADS // QUAD, QW, QW), F32)
            s_ctx = _rwkv_scan(pc, s0, *scan_args)[4]
            yf, yb, bf, bb, _ = _rwkv_scan(px, s_ctx, *scan_args)
            g2_pad = jnp.zeros((LORA_W, RW), F32).at[LORA_GATE:LORA_GATE + GATE_LORA].set(gate_g2[e]).astype(BF16)
            rwkv_x = _rwkv_finish(yf, yb, bf, bb, px, mu_lo, g2_pad, ln_w[e][None], ln_b[e][None])
            att_x = _attention(px, pc, sink[e])
            xh, aff = _outproj(rwkv_x, att_x, w_out[e].astype(BF16), x, gt_a, norm_ffn[l], sc_f, sh_f,
                               router_w[l].astype(BF16))
        else:
            o = l // 2
            xh, aff = _pool_layer(x, norm_mix[l], sc_a, sh_a, pool_w[o], pool_scale[o], gt_a, norm_ffn[l],
                                  sc_f, sh_f, router_w[l].astype(BF16))
        x = _moe_layer(xh, aff, gt_f, exp_w_gate[l], exp_w_up[l], exp_w_down[l])
    return _final_norm(x, norm_final)
```

```python
import functools

import jax
import jax.numpy as jnp
from jax import lax
from jax.experimental import pallas as pl
from jax.experimental.pallas import tpu as pltpu

F32 = jnp.float32
BF16 = jnp.bfloat16
HIGHEST = lax.Precision.HIGHEST

GRID_W = 64
HEADS = 16
HEAD = 64
RW = HEADS * HEAD
DECAY_LORA = 64
AAA_LORA = 64
GATE_LORA = 160
GN_EPS = 64e-5
KV_HEADS = 4
GROUP = HEADS // KV_HEADS
KVW = KV_HEADS * HEAD
ATT_SCALE = HEAD ** -0.5
WINDOW = 128
BLOCK = 128
ROPE_THETA = 10000.0
POOL_WINDOWS = (2, 4, 8, 16)
N_EXPERTS = 16
CAPACITY_FACTOR = 2
NORM_EPS = 1e-6

COL_RKV = 0
COL_Q = 3 * RW
COL_AK = 4 * RW
COL_AV = 4 * RW + KVW
COL_LORA = 4 * RW + 2 * KVW
LORA_W = 512
N_COLS = COL_LORA + LORA_W
LORA_GATE = 2 * DECAY_LORA + 2 * AAA_LORA
CHUNK = 64
QUAD = 4
QW = QUAD * HEAD
INTERLEAVE = 8
assert CHUNK == HEAD
HALO = 8
LANES = 128
VMEM_LIMIT = 56 * 1024 * 1024


def _cparams(sem):
    return pltpu.CompilerParams(dimension_semantics=sem, vmem_limit_bytes=VMEM_LIMIT)


def _tile(n, pref):
    t = min(n, pref)
    assert n % t == 0, (n, t)
    return t


def _moe_cap(t, ne):
    return max(1, CAPACITY_FACTOR * t // ne)


def _moe_tile(cap):
    return _tile(cap, 256)


def _row_tile(t):
    return _tile(t, 256)


def _xh_rows(t, ne):
    assert _moe_tile(_moe_cap(t, ne)) <= _row_tile(t)
    return t + _row_tile(t)


def _bdot(a, b):
    return jnp.dot(a.astype(BF16), b.astype(BF16), preferred_element_type=F32)


def _bdot_nt(a, b):
    return lax.dot_general(a.astype(BF16), b.astype(BF16), (((1,), (1,)), ((), ())),
                           preferred_element_type=F32)


def _fdot(a, b):
    return jnp.dot(a, b, preferred_element_type=F32, precision=HIGHEST)


def _rms_mod(x, g, sc, sh):
    y = x * lax.rsqrt(jnp.mean(x * x, axis=-1, keepdims=True) + NORM_EPS)
    return (y * g) * (1.0 + sc) + sh


def _head_sum(x, bd):
    return jnp.concatenate(
        [_fdot(x[:, g * LANES:(g + 1) * LANES], bd) for g in range(x.shape[1] // LANES)], axis=1)


def _block_diag_ones():
    r = lax.broadcasted_iota(jnp.int32, (LANES, LANES), 0) // HEAD
    c = lax.broadcasted_iota(jnp.int32, (LANES, LANES), 1) // HEAD
    return (r == c).astype(F32)


def _ada_kernel(cc_ref, w_ref, b_ref, o_ref):
    cc = cc_ref[...]
    s = cc * jax.nn.sigmoid(cc)
    o_ref[...] = _bdot(s, w_ref[...]) + b_ref[...]


def _ada(cc, ada_w, ada_b):
    n_l, d, n = ada_w.shape
    tn = _tile(n, 1536)
    return pl.pallas_call(
        _ada_kernel,
        grid=(n_l, n // tn),
        in_specs=[pl.BlockSpec((8, d), lambda l, j: (0, 0)),
                  pl.BlockSpec((None, d, tn), lambda l, j: (l, 0, j)),
                  pl.BlockSpec((None, 1, tn), lambda l, j: (l, 0, j))],
        out_specs=pl.BlockSpec((None, 8, tn), lambda l, j: (l, 0, j)),
        out_shape=jax.ShapeDtypeStruct((n_l, 8, n), F32),
        compiler_params=_cparams(("arbitrary", "arbitrary")),
        name="adaln",
    )(cc, ada_w, ada_b.reshape(n_l, 1, n))


def _inproj_kernel(*refs, rope_lo, rope_hi, tn):
    if rope_hi > rope_lo:
        x_ref, g_ref, sc_ref, sh_ref, w_ref, cos_ref, sin_ref, o_ref, h_scr = refs
    else:
        x_ref, g_ref, sc_ref, sh_ref, w_ref, o_ref, h_scr = refs
    j = pl.program_id(2)

    @pl.when(j == 0)
    def _():
        h_scr[...] = _rms_mod(x_ref[...], g_ref[...], sc_ref[...], sh_ref[...]).astype(BF16)

    acc = jnp.dot(h_scr[...], w_ref[...], preferred_element_type=F32)
    if rope_hi > rope_lo:
        def store_with_rope(n_groups):
            cos = cos_ref[...]
            sin = sin_ref[...]
            lane = lax.broadcasted_iota(jnp.int32, cos.shape, 1)
            first = (lane % 32) < 16
            outs = []
            for g in range(n_groups):
                a = acc[:, g * LANES:(g + 1) * LANES]
                partner = jnp.where(first, pltpu.roll(a, LANES - 16, 1), pltpu.roll(a, 16, 1))
                outs.append(a * cos + partner * sin)
            if n_groups < tn // LANES:
                outs.append(acc[:, n_groups * LANES:])
            o_ref[...] = jnp.concatenate(outs, axis=1)

        is_q = (j >= rope_lo) & (j < rope_hi)
        is_k = j == rope_hi

        @pl.when(is_q)
        def _():
            store_with_rope(tn // LANES)

        @pl.when(is_k)
        def _():
            store_with_rope(KVW // LANES)

        @pl.when(jnp.logical_not(is_q | is_k))
        def _():
            o_ref[...] = acc
    else:
        o_ref[...] = acc


def _inproj(x, g, sc, sh, w, cos_t=None, sin_t=None):
    b, t, d = x.shape
    n = w.shape[1]
    tm = _tile(t, 1024)
    tn = 512
    assert COL_Q % tn == 0 and COL_AK % tn == 0 and 2 * KVW == tn
    rope = cos_t is not None
    rope_lo, rope_hi = (COL_Q // tn, COL_AK // tn) if rope else (0, 0)
    in_specs = [pl.BlockSpec((None, tm, d), lambda bi, i, j: (bi, i, 0)),
                pl.BlockSpec((1, d), lambda bi, i, j: (0, 0)),
                pl.BlockSpec((None, 1, d), lambda bi, i, j: (bi, 0, 0)),
                pl.BlockSpec((None, 1, d), lambda bi, i, j: (bi, 0, 0)),
                pl.BlockSpec((d, tn), lambda bi, i, j: (0, j))]
    args = [x, g.reshape(1, d), sc, sh, w]
    if rope:
        in_specs += [pl.BlockSpec((tm, LANES), lambda bi, i, j: (i, 0)),
                     pl.BlockSpec((tm, LANES), lambda bi, i, j: (i, 0))]
        args += [cos_t, sin_t]
    return pl.pallas_call(
        functools.partial(_inproj_kernel, rope_lo=rope_lo, rope_hi=rope_hi, tn=tn),
        grid=(b, t // tm, n // tn),
        in_specs=in_specs,
        out_specs=pl.BlockSpec((None, tm, tn), lambda bi, i, j: (bi, i, j)),
        out_shape=jax.ShapeDtypeStruct((b, t, n), F32),
        scratch_shapes=[pltpu.VMEM((tm, d), BF16)],
        compiler_params=_cparams(("arbitrary", "arbitrary", "arbitrary")),
        name="inproj_rope" if rope else "inproj_ctx",
    )(*args)


def _shift_lerp(p, prev_blk, next_blk, mu, has_prev, has_next):
    rows = p.shape[0]
    ridx = lax.broadcasted_iota(jnp.int32, p.shape, 0)
    prev_row = jnp.where(has_prev, prev_blk[HALO - 1:HALO, :], 0.0)
    next_row = jnp.where(has_next, next_blk[0:1, :], 0.0)
    prev = jnp.where(ridx == 0, prev_row, pltpu.roll(p, 1, 0))
    nxt = jnp.where(ridx == rows - 1, next_row, pltpu.roll(p, rows - 1, 0))
    return p + mu * (0.5 * (prev + nxt) - p)


def _rwkv_kernel(*refs, n_chunks, seq_len):
    (rkv_f, lo_f, rkv_fp, lo_fp, rkv_fn, lo_fn,
     rkv_b, lo_b, rkv_bp, lo_bp, rkv_bn, lo_bn,
     mu_rkv_ref, mu_lo_ref, w0_ref, w2_ref, a0_ref, a2_ref, kk_ref, ka_ref, rk_ref, s0_ref,
     yf_ref, yb_ref, bf_ref, bb_ref, sfin_ref, state) = refs
    c = pl.program_id(1)
    cn = CHUNK

    @pl.when(c == 0)
    def _():
        state[...] = s0_ref[...]

    bd = _block_diag_ones()
    ri = lax.broadcasted_iota(jnp.int32, (cn, cn), 0)
    ci = lax.broadcasted_iota(jnp.int32, (cn, cn), 1)
    qr = lax.broadcasted_iota(jnp.int32, (QW, QW), 0)
    qc = lax.broadcasted_iota(jnp.int32, (QW, QW), 1)
    same_head = (qr // HEAD) == (qc // HEAD)
    eye = (qr == qc).astype(F32)
    lane_head = lax.broadcasted_iota(jnp.int32, (cn, QW), 1) // HEAD
    first_half = lax.broadcasted_iota(jnp.int32, (2 * QW, 2 * cn), 1) < cn
    mu_rkv = mu_rkv_ref[...]
    mu_lo = mu_lo_ref[...]
    k_k = kk_ref[...]
    k_a = ka_ref[...]
    r_k = rk_ref[...]

    dirs = ((0, c, rkv_f, lo_f, rkv_fp, lo_fp, rkv_fn, lo_fn, yf_ref, bf_ref),
            (1, n_chunks - 1 - c, rkv_b, lo_b, rkv_bp, lo_bp, rkv_bn, lo_bn, yb_ref, bb_ref))
    prep = []
    for d, cd, rkv_ref, lo_ref, rkv_p, lo_p, rkv_n, lo_n, y_ref, bon_ref in dirs:
        has_prev = cd > 0
        has_next = cd < n_chunks - 1
        m = _shift_lerp(rkv_ref[...], rkv_p[...], rkv_n[...], mu_rkv, has_prev, has_next)
        ml = _shift_lerp(lo_ref[...], lo_p[...], lo_n[...], mu_lo, has_prev, has_next)
        r = m[:, 0:RW]
        k = m[:, RW:2 * RW]
        v = m[:, 2 * RW:3 * RW]
        wd = ml[:, d * DECAY_LORA:(d + 1) * DECAY_LORA]
        ad = ml[:, 2 * DECAY_LORA + d * AAA_LORA:2 * DECAY_LORA + (d + 1) * AAA_LORA]
        w_pre = w0_ref[d] + _bdot(jnp.tanh(wd), w2_ref[d])
        z = -w_pre
        softplus = jnp.maximum(z, 0.0) + jnp.log(1.0 + jnp.exp(-jnp.abs(z)))
        logw = -jnp.exp(-softplus - 0.5)
        a = jax.nn.sigmoid(a0_ref[d] + _bdot(ad, a2_ref[d]))
        kk = k * k_k
        kk = kk * lax.rsqrt(jnp.maximum(_head_sum(kk * kk, bd), 1e-24))
        k_mod = k * (1.0 + (a - 1.0) * k_a)
        bon_ref[...] = _head_sum(r * k_mod * r_k, bd) * v

        if d == 0:
            tri = (ci <= ri).astype(F32)
            strict = (qc % cn) < (qr % cn)
            incl = (qc % cn) <= (qr % cn)
            last = cn - 1
        else:
            tri = (ci >= ri).astype(F32)
            strict = (qc % cn) > (qr % cn)
            incl = (qc % cn) >= (qr % cn)
            last = 0
        cum = _fdot(tri, logw)
        e_pos = jnp.exp(cum)
        e_neg = jnp.exp(-cum)
        p_last = e_pos[last:last + 1, :]
        bh = (kk * a) * e_neg
        kh = k_mod * e_neg
        prep.append(dict(d=d, at=(-kk) * jnp.exp(cum - logw), rt=r * e_pos, bh=bh, kh=kh, v=v,
                         bbar=bh * p_last, kbar=kh * p_last, p_last=p_last,
                         m_strict=same_head & strict, m_incl=same_head & incl, y_ref=y_ref))

    def rep(x):
        return jnp.concatenate([x] * QUAD, axis=0)

    def to_bd(x):
        return jnp.where(same_head, rep(x), 0.0)

    def collapse(x):
        out = jnp.where(lane_head == 0, x[0:cn], 0.0)
        for h in range(1, QUAD):
            out = out + jnp.where(lane_head == h, x[h * cn:(h + 1) * cn], 0.0)
        return out

    all_items = []
    for pr in prep:
        for q in range(HEADS // QUAD):
            sl = slice(q * QW, (q + 1) * QW)
            it = {name: pr[name][:, sl] for name in ("at", "rt", "bh", "kh", "v", "bbar", "kbar", "p_last")}
            it.update(d=pr["d"], q=q, m_strict=pr["m_strict"], m_incl=pr["m_incl"])
            all_items.append(it)

    def run_stages(items):
        for it in items:
            g = _bdot_nt(jnp.concatenate([to_bd(it["at"]), to_bd(it["rt"])], axis=0),
                         jnp.concatenate([it["bh"], it["kh"]], axis=0))
            g_sw = pltpu.roll(g, cn, 1)
            gb = jnp.where(first_half, g, g_sw)
            gk = jnp.where(first_half, g_sw, g)
            gb = jnp.concatenate([gb] * (QW // (2 * cn)), axis=1)
            gk = jnp.concatenate([gk] * (QW // (2 * cn)), axis=1)
            it["a_ak"] = jnp.where(it["m_strict"], gk[:QW], 0.0)
            it["a_rb"] = jnp.where(it["m_incl"], gb[QW:], 0.0)
            it["a_rk"] = jnp.where(it["m_incl"], gk[QW:], 0.0)
            it["p"] = jnp.where(it["m_strict"], gb[:QW], 0.0)
            it["t"] = eye + it["p"]
        for it in items:
            pb = it["p"].astype(BF16)
            it["p"] = jnp.dot(pb, pb, preferred_element_type=F32)
        for _ in range(4):
            for it in items:
                tp = _bdot(jnp.concatenate([it["t"], it["p"]], axis=0), it["p"])
                it["t"] = it["t"] + tp[:QW]
                it["p"] = tp[QW:]
        for it in items:
            it["t"] = it["t"] + _bdot(it["t"], it["p"])
        for it in items:
            av = _bdot(jnp.concatenate([it["a_ak"], it["a_rk"]], axis=0), rep(it["v"]))
            it["aakv"] = collapse(av[:QW])
            it["y0"] = collapse(av[QW:])
        for it in items:
            w = _bdot(it["t"], jnp.concatenate([rep(it["at"]), rep(it["aakv"])], axis=1))
            it["w1"] = collapse(w[:, :QW])
            it["w2"] = collapse(w[:, QW:])
            it["xt"] = jnp.concatenate([it["bbar"], it["kbar"]], axis=0).T
            it["pcol"] = jnp.broadcast_to(it["p_last"], (LANES, QW)).T[:, 0:1]
        for it in items:
            it["s"] = state[it["d"], it["q"]]
        for it in items:
            us0 = _bdot(jnp.concatenate([it["w1"], it["rt"]], axis=0), it["s"])
            it["u"] = us0[:cn] + it["w2"]
            it["ys"] = us0[cn:]
        for it in items:
            it["y"] = it["ys"] + collapse(_bdot(it["a_rb"], rep(it["u"]))) + it["y0"]
        for it in items:
            upd = _bdot(it["xt"], jnp.concatenate([it["u"], it["v"]], axis=0))
            state[it["d"], it["q"]] = it["s"] * it["pcol"] + jnp.where(same_head, upd, 0.0)

    for first in range(0, len(all_items), INTERLEAVE):
        run_stages(all_items[first:first + INTERLEAVE])
    for pr in prep:
        pr["y_ref"][...] = jnp.concatenate([it["y"] for it in all_items if it["d"] == pr["d"]], axis=1)

    @pl.when(c == n_chunks - 1)
    def _():
        sfin_ref[...] = state[...]


def _rwkv_scan(px, s0, mu_rkv, mu_lo, w0, w2, a0, a2, k_k, k_a, r_k):
    b, t, _ = px.shape
    cn = CHUNK
    nc = t // cn
    assert t % cn == 0 and cn % HALO == 0
    hb = cn // HALO
    n_halo = t // HALO
    lora_blk = COL_LORA // LORA_W

    def fwd(ci):
        return ci

    def bwd(ci):
        return nc - 1 - ci

    def specs(cmap):
        return [
            pl.BlockSpec((None, cn, 3 * RW), lambda bi, ci: (bi, cmap(ci), 0)),
            pl.BlockSpec((None, cn, LORA_W), lambda bi, ci: (bi, cmap(ci), lora_blk)),
            pl.BlockSpec((None, HALO, 3 * RW), lambda bi, ci: (bi, jnp.maximum(cmap(ci) * hb - 1, 0), 0)),
            pl.BlockSpec((None, HALO, LORA_W),
                         lambda bi, ci: (bi, jnp.maximum(cmap(ci) * hb - 1, 0), lora_blk)),
            pl.BlockSpec((None, HALO, 3 * RW),
                         lambda bi, ci: (bi, jnp.minimum((cmap(ci) + 1) * hb, n_halo - 1), 0)),
            pl.BlockSpec((None, HALO, LORA_W),
                         lambda bi, ci: (bi, jnp.minimum((cmap(ci) + 1) * hb, n_halo - 1), lora_blk)),
        ]

    def const(shape):
        nd = len(shape)
        return pl.BlockSpec(shape, lambda bi, ci: (0,) * nd)

    in_specs = specs(fwd) + specs(bwd) + [
        const((1, 3 * RW)), const((1, LORA_W)),
        const((2, 1, RW)), const((2, DECAY_LORA, RW)), const((2, 1, RW)), const((2, AAA_LORA, RW)),
        const((1, RW)), const((1, RW)), const((1, RW)),
        pl.BlockSpec((None, 2, HEADS // QUAD, QW, QW), lambda bi, ci: (bi, 0, 0, 0, 0)),
    ]
    yshape = jax.ShapeDtypeStruct((b, t, RW), F32)
    out_specs = [
        pl.BlockSpec((None, cn, RW), lambda bi, ci: (bi, ci, 0)),
        pl.BlockSpec((None, cn, RW), lambda bi, ci: (bi, nc - 1 - ci, 0)),
        pl.BlockSpec((None, cn, RW), lambda bi, ci: (bi, ci, 0)),
        pl.BlockSpec((None, cn, RW), lambda bi, ci: (bi, nc - 1 - ci, 0)),
        pl.BlockSpec((None, 2, HEADS // QUAD, QW, QW), lambda bi, ci: (bi, 0, 0, 0, 0)),
    ]
    px_args = [px] * 12
    return pl.pallas_call(
        functools.partial(_rwkv_kernel, n_chunks=nc, seq_len=t),
        grid=(b, nc),
        in_specs=in_specs,
        out_specs=out_specs,
        out_shape=[yshape, yshape, yshape, yshape,
                   jax.ShapeDtypeStruct((b, 2, HEADS // QUAD, QW, QW), F32)],
        scratch_shapes=[pltpu.VMEM((2, HEADS // QUAD, QW, QW), F32)],
        compiler_params=_cparams(("arbitrary", "arbitrary")),
        name="rwkv_scan",
    )(*px_args, mu_rkv, mu_lo, w0, w2.astype(BF16), a0, a2.astype(BF16), k_k, k_a, r_k, s0)


def _relayout_in_cols(a):
    n_lora = 2 * DECAY_LORA + 2 * AAA_LORA + GATE_LORA
    rkv = a[..., :3 * RW]
    lora = a[..., 3 * RW:3 * RW + n_lora]
    att = a[..., 3 * RW + n_lora:]
    pad = jnp.zeros(a.shape[:-1] + (LORA_W - n_lora,), a.dtype)
    return jnp.concatenate([rkv, att, lora, pad], axis=-1)


def _rope_tables(t):
    rows = t // GRID_W
    row = jnp.repeat(jnp.arange(rows, dtype=F32), GRID_W)
    col = jnp.tile(jnp.arange(GRID_W, dtype=F32), rows)
    n_freq = HEAD // 4
    inv = ROPE_THETA ** (-jnp.arange(n_freq, dtype=F32) / n_freq)
    ar = row[:, None] * inv
    ac = col[:, None] * inv
    cos = jnp.concatenate([jnp.cos(ar), jnp.cos(ar), jnp.cos(ac), jnp.cos(ac)], axis=-1)
    sin = jnp.concatenate([-jnp.sin(ar), jnp.sin(ar), -jnp.sin(ac), jnp.sin(ac)], axis=-1)
    reps = LANES // HEAD
    return jnp.tile(cos, (1, reps)), jnp.tile(sin, (1, reps))


def _finish_kernel(yf_ref, yb_ref, bf_ref, bb_ref, lo_ref, lo_p, lo_n, mu_lo_ref, g2_ref, lnw_ref, lnb_ref,
                   o_ref, *, n_tiles):
    i = pl.program_id(1)
    bd = _block_diag_ones()
    y = yf_ref[...] + yb_ref[...]
    mean = _head_sum(y, bd) * (1.0 / HEAD)
    yc = y - mean
    var = _head_sum(yc * yc, bd) * (1.0 / HEAD)
    yn = yc * lax.rsqrt(var + GN_EPS) * lnw_ref[...] + lnb_ref[...]
    ml = _shift_lerp(lo_ref[...], lo_p[...], lo_n[...], mu_lo_ref[...], i > 0, i < n_tiles - 1)
    gate = _bdot(jax.nn.sigmoid(ml), g2_ref[...])
    o_ref[...] = ((yn + bf_ref[...] + bb_ref[...]) * gate).astype(o_ref.dtype)


def _rwkv_finish(yf, yb, bf, bb, px, mu_lo, g2_pad, ln_w, ln_b):
    b, t, _ = yf.shape
    tm = _tile(t, 256)
    nt = t // tm
    hb = tm // HALO
    n_halo = t // HALO
    lora_blk = COL_LORA // LORA_W
    tile = pl.BlockSpec((None, tm, RW), lambda bi, i: (bi, i, 0))
    vec = pl.BlockSpec((1, RW), lambda bi, i: (0, 0))
    return pl.pallas_call(
        functools.partial(_finish_kernel, n_tiles=nt),
        grid=(b, nt),
        in_specs=[tile, tile, tile, tile,
                  pl.BlockSpec((None, tm, LORA_W), lambda bi, i: (bi, i, lora_blk)),
                  pl.BlockSpec((None, HALO, LORA_W), lambda bi, i: (bi, jnp.maximum(i * hb - 1, 0), lora_blk)),
                  pl.BlockSpec((None, HALO, LORA_W),
                               lambda bi, i: (bi, jnp.minimum((i + 1) * hb, n_halo - 1), lora_blk)),
                  pl.BlockSpec((1, LORA_W), lambda bi, i: (0, 0)),
                  pl.BlockSpec((LORA_W, RW), lambda bi, i: (0, 0)),
                  vec, vec],
        out_specs=tile,
        out_shape=jax.ShapeDtypeStruct((b, t, RW), BF16),
        compiler_params=_cparams(("arbitrary", "arbitrary")),
        name="rwkv_finish",
    )(yf, yb, bf, bb, px, px, px, mu_lo, g2_pad, ln_w, ln_b)


def _attn_kernel(q_ref, kp_ref, kc_ref, kn_ref, vp_ref, vc_ref, vn_ref, ck_ref, cv_ref, sink_ref, o_ref,
                 *, n_blocks, n_ctx):
    n = pl.program_id(1)
    q = q_ref[...] * ATT_SCALE
    k_win = jnp.concatenate([kp_ref[...], kc_ref[...], kn_ref[...]], axis=0)
    v_win = jnp.concatenate([vp_ref[...], vc_ref[...], vn_ref[...]], axis=0)
    k_all = jnp.concatenate([ck_ref[...], k_win], axis=0).astype(BF16)
    v_all = jnp.concatenate([cv_ref[...], v_win], axis=0).astype(BF16)
    rows = GROUP * BLOCK
    span = n_ctx + 3 * BLOCK
    qi = lax.broadcasted_iota(jnp.int32, (rows, span), 0) % BLOCK + BLOCK
    kj = lax.broadcasted_iota(jnp.int32, (rows, span), 1) - n_ctx
    k_lo = jnp.where(n == 0, BLOCK, 0)
    k_hi = jnp.where(n == n_blocks - 1, 2 * BLOCK, 3 * BLOCK)
    in_win = (jnp.abs(qi - kj) <= WINDOW) & (kj >= k_lo) & (kj < k_hi)
    ok = (kj < 0) | in_win
    sink = sink_ref[...]
    outs = []
    for g in range(KV_HEADS):
        qg = jnp.concatenate([q[:, (g * GROUP + i) * HEAD:(g * GROUP + i + 1) * HEAD] for i in range(GROUP)],
                             axis=0)
        kg = k_all[:, g * HEAD:(g + 1) * HEAD]
        vg = jnp.concatenate([v_all[:, g * HEAD:(g + 1) * HEAD], jnp.ones((span, HEAD), BF16)], axis=1)
        s = jnp.where(ok, _bdot_nt(qg, kg), -jnp.inf)
        sk = jnp.concatenate([jnp.broadcast_to(sink[:, g * GROUP + i:g * GROUP + i + 1], (BLOCK, 1))
                              for i in range(GROUP)], axis=0)
        m = jnp.maximum(jnp.max(s, axis=-1, keepdims=True), sk)
        p = jnp.exp((s - m).astype(BF16))
        og = jnp.dot(p, vg, preferred_element_type=F32)
        denom = og[:, HEAD:HEAD + 1] + jnp.exp(sk - m)
        og = og[:, :HEAD] / denom
        outs += [og[i * BLOCK:(i + 1) * BLOCK] for i in range(GROUP)]
    o_ref[...] = jnp.concatenate(outs, axis=1).astype(o_ref.dtype)


def _attention(px, pc, sink):
    b, t, _ = px.shape
    n_ctx = pc.shape[1]
    nb = t // BLOCK
    kb = COL_AK // KVW
    vb = COL_AV // KVW

    def blk(col, off):
        return pl.BlockSpec((None, BLOCK, KVW), lambda bi, n: (bi, jnp.clip(n + off, 0, nb - 1), col))

    return pl.pallas_call(
        functools.partial(_attn_kernel, n_blocks=nb, n_ctx=n_ctx),
        grid=(b, nb),
        in_specs=[pl.BlockSpec((None, BLOCK, RW), lambda bi, n: (bi, n, COL_Q // RW)),
                  blk(kb, -1), blk(kb, 0), blk(kb, 1), blk(vb, -1), blk(vb, 0), blk(vb, 1),
                  pl.BlockSpec((None, n_ctx, KVW), lambda bi, n: (bi, 0, kb)),
                  pl.BlockSpec((None, n_ctx, KVW), lambda bi, n: (bi, 0, vb)),
                  pl.BlockSpec((1, HEADS), lambda bi, n: (0, 0))],
        out_specs=pl.BlockSpec((None, BLOCK, RW), lambda bi, n: (bi, n, 0)),
        out_shape=jax.ShapeDtypeStruct((b, t, RW), BF16),
        compiler_params=_cparams(("arbitrary", "arbitrary")),
        name="window_attention",
    )(px, px, px, px, px, px, px, pc, pc, sink.reshape(1, HEADS))


def _ffn_pre(x_new, gf_ref, scf_ref, shf_ref, rw_ref, xh_out, aff_out):
    d = x_new.shape[1]
    xh_out[:, 0:d] = x_new
    hf = _rms_mod(x_new, gf_ref[...], scf_ref[...], shf_ref[...])
    xh_out[:, d:2 * d] = hf
    logits = _bdot(hf, rw_ref[...])
    e = jnp.exp(logits - jnp.max(logits, axis=-1, keepdims=True))
    aff_out[...] = e / jnp.sum(e, axis=-1, keepdims=True)


def _outproj_kernel(ra_ref, at_ref, w_ref, x_ref, gt_ref, gf_ref, scf_ref, shf_ref, rw_ref,
                    xh_out, aff_out, *, n_tiles):
    i = pl.program_id(1)

    @pl.when(i < n_tiles)
    def _():
        y = (jnp.dot(ra_ref[...], w_ref[0:RW, :], preferred_element_type=F32)
             + jnp.dot(at_ref[...], w_ref[RW:2 * RW, :], preferred_element_type=F32))
        _ffn_pre(x_ref[...] + gt_ref[...] * y, gf_ref, scf_ref, shf_ref, rw_ref, xh_out, aff_out)

    @pl.when(i == n_tiles)
    def _():
        xh_out[...] = jnp.zeros(xh_out.shape, F32)


def _mod_specs(d):
    return pl.BlockSpec((None, 1, d), lambda bi, i: (bi, 0, 0))


def _outproj(rwkv_x, att_x, w_out, x, gt_a, g_ffn, sc_f, sh_f, router_w):
    b, t, d = x.shape
    ne = router_w.shape[1]
    tm = _row_tile(t)
    nt = t // tm
    assert _xh_rows(t, ne) == t + tm

    def row(bi, i):
        return (bi, jnp.minimum(i, nt - 1), 0)

    xt = pl.BlockSpec((None, tm, d), row)
    mt = pl.BlockSpec((None, tm, RW), row)
    return pl.pallas_call(
        functools.partial(_outproj_kernel, n_tiles=nt),
        grid=(b, nt + 1),
        in_specs=[mt, mt, pl.BlockSpec((2 * RW, d), lambda bi, i: (0, 0)), xt, _mod_specs(d),
                  pl.BlockSpec((1, d), lambda bi, i: (0, 0)), _mod_specs(d), _mod_specs(d),
                  pl.BlockSpec((d, ne), lambda bi, i: (0, 0))],
        out_specs=[pl.BlockSpec((None, tm, 2 * d), lambda bi, i: (bi, i, 0)),
                   pl.BlockSpec((None, tm, ne), row)],
        out_shape=[jax.ShapeDtypeStruct((b, _xh_rows(t, ne), 2 * d), F32), jax.ShapeDtypeStruct((b, t, ne), F32)],
        compiler_params=_cparams(("arbitrary", "arbitrary")),
        name="outproj_ffnpre",
    )(rwkv_x, att_x, w_out, x, gt_a, g_ffn.reshape(1, d), sc_f, sh_f, router_w)


def _pool_kernel(*refs, tm, seq_len):
    i = pl.program_id(1)
    xh_out = refs[13]

    @pl.when(i < seq_len // tm)
    def _():
        _pool_tile(*refs, tm=tm, seq_len=seq_len)

    @pl.when(i == seq_len // tm)
    def _():
        xh_out[...] = jnp.zeros(xh_out.shape, F32)


def _pool_tile(x_ref, xp_ref, xn_ref, g_ref, sc_ref, sh_ref, pw_ref, ps_ref, gt_ref,
               gf_ref, scf_ref, shf_ref, rw_ref, xh_out, aff_out, ext, *, tm, seq_len):
    i = pl.program_id(1)
    n_tiles = seq_len // tm
    x = x_ref[...]
    d = x.shape[1]
    grp = d // len(POOL_WINDOWS)
    h = _rms_mod(x, g_ref[...], sc_ref[...], sh_ref[...])
    hp = _rms_mod(xp_ref[...], g_ref[...], sc_ref[...], sh_ref[...])
    hn = _rms_mod(xn_ref[...], g_ref[...], sc_ref[...], sh_ref[...])
    ext[0:HALO, :] = jnp.where(i > 0, hp, 0.0)
    ext[HALO:HALO + tm, :] = h
    ext[HALO + tm:2 * HALO + tm, :] = jnp.where(i < n_tiles - 1, hn, 0.0)
    pos = i * tm + lax.broadcasted_iota(jnp.int32, (tm, 1), 0)
    outs = []
    for gi, w in enumerate(POOL_WINDOWS):
        cols = slice(gi * grp, (gi + 1) * grp)
        s = ext[HALO - w // 2:HALO - w // 2 + tm, cols]
        for j in range(-w // 2 + 1, w // 2):
            s = s + ext[HALO + j:HALO + j + tm, cols]
        cnt = (jnp.minimum(pos + w // 2, seq_len) - jnp.maximum(pos - w // 2, 0)).astype(F32)
        outs.append(_bdot(s / cnt - h[:, cols], pw_ref[gi]))
    y = jnp.concatenate(outs, axis=1) * ps_ref[...]
    _ffn_pre(x + gt_ref[...] * y, gf_ref, scf_ref, shf_ref, rw_ref, xh_out, aff_out)


def _pool_layer(x, t, g_mix, sc_a, sh_a, pool_w, pool_scale, gt_a, g_ffn, sc_f, sh_f, router_w):
    b = x.shape[0]
    d = g_mix.shape[0]
    ne = router_w.shape[1]
    ng, grp, _ = pool_w.shape
    assert max(POOL_WINDOWS) // 2 <= HALO
    tm = _row_tile(t)
    nt = t // tm
    assert _xh_rows(t, ne) == t + tm
    hb = tm // HALO
    n_halo = t // HALO

    def row(bi, i):
        return (bi, jnp.minimum(i, nt - 1), 0)

    xt = pl.BlockSpec((None, tm, d), row)
    vec = pl.BlockSpec((1, d), lambda bi, i: (0, 0))
    return pl.pallas_call(
        functools.partial(_pool_kernel, tm=tm, seq_len=t),
        grid=(b, nt + 1),
        in_specs=[xt,
                  pl.BlockSpec((None, HALO, d), lambda bi, i: (bi, jnp.maximum(i * hb - 1, 0), 0)),
                  pl.BlockSpec((None, HALO, d), lambda bi, i: (bi, jnp.minimum((i + 1) * hb, n_halo - 1), 0)),
                  vec, _mod_specs(d), _mod_specs(d),
                  pl.BlockSpec((ng, grp, grp), lambda bi, i: (0, 0, 0)), vec, _mod_specs(d),
                  vec, _mod_specs(d), _mod_specs(d), pl.BlockSpec((d, ne), lambda bi, i: (0, 0))],
        out_specs=[pl.BlockSpec((None, tm, 2 * d), lambda bi, i: (bi, i, 0)),
                   pl.BlockSpec((None, tm, ne), row)],
        out_shape=[jax.ShapeDtypeStruct((b, _xh_rows(t, ne), 2 * d), F32), jax.ShapeDtypeStruct((b, t, ne), F32)],
        scratch_shapes=[pltpu.VMEM((tm + 2 * HALO, d), F32)],
        compiler_params=_cparams(("arbitrary", "arbitrary")),
        name="pool_ffnpre",
    )(x, x, x, g_mix.reshape(1, d), sc_a, sh_a, pool_w.astype(BF16), pool_scale.reshape(1, d), gt_a,
      g_ffn.reshape(1, d), sc_f, sh_f, router_w)


def _route_kernel(aff_ref, idx_ref, gate_ref, *, cap):
    nr = aff_ref.shape[0]
    aff = aff_ref[...]
    capf = float(cap)

    def bit_step(i, thr):
        cand = thr | (jnp.int32(1) << (30 - i))
        cnt = jnp.sum(jnp.where(aff >= pltpu.bitcast(cand, F32), 1.0, 0.0), keepdims=True)
        return jnp.where(cnt >= capf, cand, thr)

    thr = lax.fori_loop(0, 31, bit_step, jnp.zeros((1, 1), jnp.int32))
    gt = aff >= pltpu.bitcast(thr + 1, F32)
    eq = (aff >= pltpu.bitcast(thr, F32)) & jnp.logical_not(gt)
    need = capf - jnp.sum(jnp.where(gt, 1.0, 0.0), keepdims=True)

    ri = lax.broadcasted_iota(jnp.int32, (LANES, LANES), 0)
    ci = lax.broadcasted_iota(jnp.int32, (LANES, LANES), 1)
    upper = (ri <= ci).astype(BF16)
    rr = lax.broadcasted_iota(jnp.int32, (nr, nr), 0)
    rc = lax.broadcasted_iota(jnp.int32, (nr, nr), 1)
    lower_strict = (rc < rr).astype(BF16)
    ones8 = jnp.ones((8, LANES), BF16)
    p_iota = lax.broadcasted_iota(jnp.int32, (cap, nr), 0).astype(F32)
    r_iota = lax.broadcasted_iota(jnp.int32, (cap, nr), 1).astype(F32)
    c_iota = lax.broadcasted_iota(jnp.int32, (cap, LANES), 1).astype(F32)
    p_col = lax.broadcasted_iota(jnp.int32, (cap, 1), 0).astype(F32)
    upper_r = (rr <= rc).astype(BF16)

    eq_b = jnp.where(eq, 1.0, 0.0).astype(BF16)
    eq_rank = (jnp.dot(eq_b, upper, preferred_element_type=F32) - eq_b.astype(F32)
               + jnp.sum(jnp.dot(lower_strict, eq_b, preferred_element_type=F32), axis=1, keepdims=True))
    sel = gt | (eq & (eq_rank < need))
    sel_b = jnp.where(sel, 1.0, 0.0).astype(BF16)
    lc = jnp.dot(sel_b, upper, preferred_element_type=F32)
    row_tot = lax.dot_general(ones8, sel_b, (((1,), (1,)), ((), ())),
                              preferred_element_type=F32)[0:1, :]
    row_incl = jnp.dot(row_tot.astype(BF16), upper_r, preferred_element_type=F32)
    row_excl = row_incl - row_tot
    hit = (row_excl <= p_iota) & (p_iota < row_incl)
    hit_b = jnp.where(hit, 1.0, 0.0).astype(BF16)
    r_of_p = jnp.sum(jnp.where(hit, r_iota, 0.0), axis=1, keepdims=True)
    base = jnp.sum(jnp.where(hit, row_excl, 0.0), axis=1, keepdims=True)
    lc_row = jnp.dot(hit_b, lc.astype(BF16), preferred_element_type=F32)
    c_of_p = jnp.sum(jnp.where(lc_row <= (p_col - base), 1.0, 0.0), axis=1, keepdims=True)
    idx_ref[...] = (r_of_p * float(LANES) + c_of_p).astype(jnp.int32)
    a1 = aff.astype(BF16)
    a2 = (aff - a1.astype(F32)).astype(BF16)
    a3 = (aff - a1.astype(F32) - a2.astype(F32)).astype(BF16)
    a_row = (jnp.dot(hit_b, a1, preferred_element_type=F32) + jnp.dot(hit_b, a2, preferred_element_type=F32)
             + jnp.dot(hit_b, a3, preferred_element_type=F32))
    gate_ref[...] = jnp.sum(jnp.where(c_iota == c_of_p, a_row, 0.0), axis=1, keepdims=True)


def _route(aff):
    b, t, ne = aff.shape
    cap = _moe_cap(t, ne)
    assert t % LANES == 0
    nr = t // LANES
    aff_t = jnp.swapaxes(aff, 1, 2).reshape(b, ne, nr, LANES)
    out_blk = pl.BlockSpec((None, None, cap, 1), lambda bi, e: (bi, e, 0, 0))
    idx, gate = pl.pallas_call(
        functools.partial(_route_kernel, cap=cap),
        grid=(b, ne),
        in_specs=[pl.BlockSpec((None, None, nr, LANES), lambda bi, e: (bi, e, 0, 0))],
        out_specs=[out_blk, out_blk],
        out_shape=[jax.ShapeDtypeStruct((b, ne, cap, 1), jnp.int32),
                   jax.ShapeDtypeStruct((b, ne, cap, 1), F32)],
        compiler_params=_cparams(("arbitrary", "arbitrary")),
        name="expert_choice_route",
    )(aff_t)
    return idx.reshape(b, ne, cap), gate


def _moe_kernel(idx_ref, gate_ref, gt_ref, xh_in, wg_ref, wu_ref, wd_ref, xh_hbm,
                buf, obuf, hbuf, gsem, osem, *, tile, n_exp, n_tiles, n_steps, d, seq_len):
    del xh_in
    bi = pl.program_id(0)
    e = pl.program_id(1)
    ti = pl.program_id(2)
    per_batch = n_exp * n_tiles
    step = bi * per_batch + e * n_tiles + ti
    slot = step % 2

    def row_in(tok, k, b_, slot_):
        return pltpu.make_async_copy(xh_hbm.at[b_, pl.ds(tok, 1)], buf.at[slot_, pl.ds(k, 1)], gsem.at[slot_])

    def row_out(tok, k, b_, slot_):
        return pltpu.make_async_copy(obuf.at[slot_, pl.ds(k, 1)], xh_hbm.at[b_, pl.ds(tok, 1), pl.ds(0, d)],
                                     osem.at[slot_])

    def wait_rows_in(slot_):
        pltpu.make_async_copy(xh_hbm.at[0, pl.ds(0, tile)], buf.at[slot_], gsem.at[slot_]).wait()

    def wait_rows_out(slot_):
        pltpu.make_async_copy(obuf.at[slot_], xh_hbm.at[0, pl.ds(0, tile), pl.ds(0, d)], osem.at[slot_]).wait()

    def fetch_now():
        def body(k, carry):
            row_in(idx_ref[step * tile + k], k, bi, slot).start()
            return carry

        lax.fori_loop(0, tile, body, 0, unroll=8)

    starts_expert = (ti == 0) & (e > 0)
    ends_expert = ((ti == n_tiles - 1) & (e < n_exp - 1)) | (step == n_steps - 1)

    @pl.when(starts_expert)
    def _():
        wait_rows_out(slot)
        wait_rows_out(1 - slot)
        fetch_now()

    @pl.when(step == 0)
    def _():
        obuf[...] = jnp.zeros(obuf.shape, F32)
        fetch_now()

    wait_rows_in(slot)
    hbuf[...] = buf[slot, :, d:2 * d].astype(BF16)

    src_step = jnp.where(ends_expert, step, step + 1)
    src_b = src_step // per_batch
    for k in range(tile):
        row_in(idx_ref[src_step * tile + k], k, src_b, 1 - slot).start()
    prev_written = starts_expert | (step == 0)
    prev_step = jnp.maximum(step - 1, 0)
    prev_b = prev_step // per_batch
    for k in range(tile):
        tok = jnp.where(prev_written, seq_len + k, idx_ref[prev_step * tile + k])
        row_out(tok, k, prev_b, 1 - slot).start()

    h = hbuf[...]
    g = jnp.dot(h, wg_ref[...], preferred_element_type=F32)
    u = jnp.dot(h, wu_ref[...], preferred_element_type=F32)
    hid = (g * jax.nn.sigmoid(g)) * u
    y = jnp.dot(hid.astype(BF16), wd_ref[...], preferred_element_type=F32)
    new_rows = buf[slot, :, 0:d] + y * gate_ref[...] * gt_ref[...]

    @pl.when((step > 0) & jnp.logical_not(starts_expert))
    def _():
        wait_rows_out(slot)

    obuf[slot] = new_rows

    @pl.when(ends_expert)
    def _():
        wait_rows_in(1 - slot)

        def put(k, carry):
            row_out(idx_ref[step * tile + k], k, bi, slot).start()
            return carry

        lax.fori_loop(0, tile, put, 0, unroll=8)

    @pl.when(step == n_steps - 1)
    def _():
        wait_rows_out(slot)
        wait_rows_out(1 - slot)


def _moe(xh, seq_len, idx, gate, gt_f, w_gate, w_up, w_down, layer):
    b, t_pad, d2 = xh.shape
    d = d2 // 2
    _, ne, _, ff = w_gate.shape
    cap = idx.shape[2]
    tile = _moe_tile(cap)
    assert cap == _moe_cap(seq_len, ne) and t_pad == _xh_rows(seq_len, ne)
    nt = cap // tile
    n_steps = b * ne * nt
    grid_spec = pltpu.PrefetchScalarGridSpec(
        num_scalar_prefetch=1,
        grid=(b, ne, nt),
        in_specs=[pl.BlockSpec((None, None, tile, 1), lambda bi, e, ti, idx_r: (bi, e, ti, 0)),
                  pl.BlockSpec((None, 1, d), lambda bi, e, ti, idx_r: (bi, 0, 0)),
                  pl.BlockSpec(memory_space=pl.ANY),
                  pl.BlockSpec((None, None, d, ff), lambda bi, e, ti, idx_r: (layer, e, 0, 0)),
                  pl.BlockSpec((None, None, d, ff), lambda bi, e, ti, idx_r: (layer, e, 0, 0)),
                  pl.BlockSpec((None, None, ff, d), lambda bi, e, ti, idx_r: (layer, e, 0, 0))],
        out_specs=pl.BlockSpec(memory_space=pl.ANY),
        scratch_shapes=[pltpu.VMEM((2, tile, d2), F32), pltpu.VMEM((2, tile, d), F32), pltpu.VMEM((tile, d), BF16),
                        pltpu.SemaphoreType.DMA((2,)), pltpu.SemaphoreType.DMA((2,))],
    )
    return pl.pallas_call(
        functools.partial(_moe_kernel, tile=tile, n_exp=ne, n_tiles=nt, n_steps=n_steps, d=d, seq_len=seq_len),
        grid_spec=grid_spec,
        out_shape=jax.ShapeDtypeStruct((b, t_pad, d2), F32),
        input_output_aliases={3: 0},
        compiler_params=_cparams(("arbitrary", "arbitrary", "arbitrary")),
        name="expert_ffn",
    )(idx.reshape(-1), gate, gt_f, xh, w_gate, w_up, w_down)


def _final_norm_kernel(x_ref, g_ref, o_ref):
    x = x_ref[...]
    o_ref[...] = x * lax.rsqrt(jnp.mean(x * x, axis=-1, keepdims=True) + NORM_EPS) * g_ref[...]


def _final_norm(x, t, g):
    b = x.shape[0]
    d = g.shape[0]
    tm = _tile(t, 512)
    xt = pl.BlockSpec((None, tm, d), lambda bi, i: (bi, i, 0))
    return pl.pallas_call(
        _final_norm_kernel,
        grid=(b, t // tm),
        in_specs=[xt, pl.BlockSpec((1, d), lambda bi, i: (0, 0))],
        out_specs=xt,
        out_shape=jax.ShapeDtypeStruct((b, t, d), F32),
        compiler_params=_cparams(("arbitrary", "arbitrary")),
        name="final_norm",
    )(x, g.reshape(1, d))


def _moe_layer(xh, aff, gt_f, w_gate, w_up, w_down, layer):
    idx, gate = _route(aff)
    return _moe(xh, aff.shape[1], idx, gate, gt_f, w_gate, w_up, w_down, layer)


def kernel(x, c, ctx, c_ctx, ada_w, ada_b, norm_mix, norm_ffn, w_in, shift_mu, decay_w0, decay_w2, iclr_a0,
           iclr_a2, gate_g2, k_k, k_a, r_k, ln_w, ln_b, sink, w_out, pool_w, pool_scale, router_w, exp_w_gate,
           exp_w_up, exp_w_down, norm_final):
    b, t, d = x.shape
    depth = ada_w.shape[0]
    assert depth == 2, "the context stream is only advanced for deeper stacks; not implemented"
    assert b + 1 <= 8
    cc = jnp.concatenate([c, c_ctx[None, :], jnp.zeros((8 - b - 1, d), F32)], axis=0)
    mod = _ada(cc, ada_w, ada_b)

    def mods(l, rows):
        return [mod[l, rows, None, i * d:(i + 1) * d] for i in range(6)]

    cos_t, sin_t = _rope_tables(t)
    wg_b, wu_b, wd_b = exp_w_gate.astype(BF16), exp_w_up.astype(BF16), exp_w_down.astype(BF16)
    for l in range(depth):
        sh_a, sc_a, gt_a, sh_f, sc_f, gt_f = mods(l, slice(0, b))
        if l % 2 == 0:
            e = l // 2
            csh_a, csc_a = [jnp.broadcast_to(m, (b, 1, d)) for m in mods(l, slice(b, b + 1))[:2]]
            w_p = _relayout_in_cols(w_in[e]).astype(BF16)
            mu_p = _relayout_in_cols(jnp.concatenate(
                [shift_mu[e], jnp.zeros((w_in.shape[2] - shift_mu.shape[1],), F32)]))
            mu_rkv = mu_p[None, :3 * RW]
            mu_lo = mu_p[None, COL_LORA:]
            px = _inproj(x, norm_mix[l], sc_a, sh_a, w_p, cos_t, sin_t)
            pc = _inproj(ctx, norm_mix[l], csc_a, csh_a, w_p)
            scan_args = (mu_rkv, mu_lo, decay_w0[e][:, None, :], decay_w2[e], iclr_a0[e][:, None, :], iclr_a2[e],
                         k_k[e][None], k_a[e][None], r_k[e][None])
            s0 = jnp.zeros((b, 2, HEADS // QUAD, QW, QW), F32)
            s_ctx = _rwkv_scan(pc, s0, *scan_args)[4]
            yf, yb, bf, bb, _ = _rwkv_scan(px, s_ctx, *scan_args)
            g2_pad = jnp.zeros((LORA_W, RW), F32).at[LORA_GATE:LORA_GATE + GATE_LORA].set(gate_g2[e]).astype(BF16)
            rwkv_x = _rwkv_finish(yf, yb, bf, bb, px, mu_lo, g2_pad, ln_w[e][None], ln_b[e][None])
            att_x = _attention(px, pc, sink[e])
            xh, aff = _outproj(rwkv_x, att_x, w_out[e].astype(BF16), x, gt_a, norm_ffn[l], sc_f, sh_f,
                               router_w[l].astype(BF16))
        else:
            o = l // 2
            xh, aff = _pool_layer(x, t, norm_mix[l], sc_a, sh_a, pool_w[o], pool_scale[o], gt_a, norm_ffn[l],
                                  sc_f, sh_f, router_w[l].astype(BF16))
        x = _moe_layer(xh, aff, gt_f, wg_b, wu_b, wd_b, l)
    return _final_norm(x, t, norm_final)
```

```python
import functools

import jax
import jax.numpy as jnp
from jax import lax
from jax.experimental import pallas as pl
from jax.experimental.pallas import tpu as pltpu

F32 = jnp.float32
BF16 = jnp.bfloat16

GRID_W = 64
HEADS = 16
HEAD = 64
RW = HEADS * HEAD
DECAY_LORA = 64
AAA_LORA = 64
GATE_LORA = 160
GN_EPS = 64e-5
KV_HEADS = 4
GROUP = HEADS // KV_HEADS
KVW = KV_HEADS * HEAD
ATT_SCALE = HEAD ** -0.5
WINDOW = 128
BLOCK = 128
ROPE_THETA = 10000.0
POOL_WINDOWS = (2, 4, 8, 16)
N_EXPERTS = 16
CAPACITY_FACTOR = 2
NORM_EPS = 1e-6

COL_RKV = 0
COL_Q = 3 * RW
COL_AK = 4 * RW
COL_AV = 4 * RW + KVW
COL_LORA = 4 * RW + 2 * KVW
LORA_W = 512
N_COLS = COL_LORA + LORA_W
LORA_GATE = 2 * DECAY_LORA + 2 * AAA_LORA
CHUNK = 64
QUAD = 4
QW = QUAD * HEAD
INTERLEAVE = 8
assert CHUNK == HEAD
HALO = 8
LANES = 128
VMEM_LIMIT = 56 * 1024 * 1024


def _cparams(sem):
    return pltpu.CompilerParams(dimension_semantics=sem, vmem_limit_bytes=VMEM_LIMIT)


def _tile(n, pref):
    t = min(n, pref)
    assert n % t == 0, (n, t)
    return t


def _moe_cap(t, ne):
    return max(1, CAPACITY_FACTOR * t // ne)


def _moe_tile(cap):
    return _tile(cap, 256)


def _row_tile(t):
    return _tile(t, 256)


def _xh_rows(t, ne):
    assert _moe_tile(_moe_cap(t, ne)) <= _row_tile(t)
    return t + _row_tile(t)


def _bdot(a, b):
    return jnp.dot(a.astype(BF16), b.astype(BF16), preferred_element_type=F32)


def _bdot_nt(a, b):
    return lax.dot_general(a.astype(BF16), b.astype(BF16), (((1,), (1,)), ((), ())),
                           preferred_element_type=F32)


def _bf16_terms(x, n):
    terms = []
    for _ in range(n):
        t = x.astype(BF16)
        terms.append(t)
        x = x - t.astype(F32)
    return terms


def _rms_mod(x, g, sc, sh):
    y = x * lax.rsqrt(jnp.mean(x * x, axis=-1, keepdims=True) + NORM_EPS)
    return (y * g) * (1.0 + sc) + sh


def _head_sum(x, bd):
    cols = []
    for g in range(x.shape[1] // LANES):
        cols.append(sum(jnp.dot(t, bd, preferred_element_type=F32)
                        for t in _bf16_terms(x[:, g * LANES:(g + 1) * LANES], 2)))
    return jnp.concatenate(cols, axis=1)


def _block_diag_ones():
    r = lax.broadcasted_iota(jnp.int32, (LANES, LANES), 0) // HEAD
    c = lax.broadcasted_iota(jnp.int32, (LANES, LANES), 1) // HEAD
    return jnp.where(r == c, 1.0, 0.0).astype(BF16)


def _ada_kernel(cc_ref, w_ref, b_ref, o_ref):
    cc = cc_ref[...]
    s = cc * jax.nn.sigmoid(cc)
    o_ref[...] = _bdot(s, w_ref[...]) + b_ref[...]


def _ada(cc, ada_w, ada_b):
    n_l, d, n = ada_w.shape
    tn = _tile(n, 1536)
    return pl.pallas_call(
        _ada_kernel,
        grid=(n_l, n // tn),
        in_specs=[pl.BlockSpec((8, d), lambda l, j: (0, 0)),
                  pl.BlockSpec((None, d, tn), lambda l, j: (l, 0, j)),
                  pl.BlockSpec((None, 1, tn), lambda l, j: (l, 0, j))],
        out_specs=pl.BlockSpec((None, 8, tn), lambda l, j: (l, 0, j)),
        out_shape=jax.ShapeDtypeStruct((n_l, 8, n), F32),
        compiler_params=_cparams(("arbitrary", "arbitrary")),
        name="adaln",
    )(cc, ada_w, ada_b.reshape(n_l, 1, n))


def _inproj_kernel(*refs, rope_lo, rope_hi, tn):
    if rope_hi > rope_lo:
        x_ref, g_ref, sc_ref, sh_ref, w_ref, cos_ref, sin_ref, o_ref, h_scr = refs
    else:
        x_ref, g_ref, sc_ref, sh_ref, w_ref, o_ref, h_scr = refs
    j = pl.program_id(2)

    @pl.when(j == 0)
    def _():
        h_scr[...] = _rms_mod(x_ref[...], g_ref[...], sc_ref[...], sh_ref[...]).astype(BF16)

    acc = jnp.dot(h_scr[...], w_ref[...], preferred_element_type=F32)
    if rope_hi > rope_lo:
        def store_with_rope(n_groups):
            cos = cos_ref[...]
            sin = sin_ref[...]
            lane = lax.broadcasted_iota(jnp.int32, cos.shape, 1)
            first = (lane % 32) < 16
            outs = []
            for g in range(n_groups):
                a = acc[:, g * LANES:(g + 1) * LANES]
                partner = jnp.where(first, pltpu.roll(a, LANES - 16, 1), pltpu.roll(a, 16, 1))
                outs.append(a * cos + partner * sin)
            if n_groups < tn // LANES:
                outs.append(acc[:, n_groups * LANES:])
            o_ref[...] = jnp.concatenate(outs, axis=1)

        is_q = (j >= rope_lo) & (j < rope_hi)
        is_k = j == rope_hi

        @pl.when(is_q)
        def _():
            store_with_rope(tn // LANES)

        @pl.when(is_k)
        def _():
            store_with_rope(KVW // LANES)

        @pl.when(jnp.logical_not(is_q | is_k))
        def _():
            o_ref[...] = acc
    else:
        o_ref[...] = acc


def _inproj(x, g, sc, sh, w, cos_t=None, sin_t=None):
    b, t, d = x.shape
    n = w.shape[1]
    tm = _tile(t, 1024)
    tn = 512
    assert COL_Q % tn == 0 and COL_AK % tn == 0 and 2 * KVW == tn
    rope = cos_t is not None
    rope_lo, rope_hi = (COL_Q // tn, COL_AK // tn) if rope else (0, 0)
    in_specs = [pl.BlockSpec((None, tm, d), lambda bi, i, j: (bi, i, 0)),
                pl.BlockSpec((1, d), lambda bi, i, j: (0, 0)),
                pl.BlockSpec((None, 1, d), lambda bi, i, j: (bi, 0, 0)),
                pl.BlockSpec((None, 1, d), lambda bi, i, j: (bi, 0, 0)),
                pl.BlockSpec((d, tn), lambda bi, i, j: (0, j))]
    args = [x, g.reshape(1, d), sc, sh, w]
    if rope:
        in_specs += [pl.BlockSpec((tm, LANES), lambda bi, i, j: (i, 0)),
                     pl.BlockSpec((tm, LANES), lambda bi, i, j: (i, 0))]
        args += [cos_t, sin_t]
    return pl.pallas_call(
        functools.partial(_inproj_kernel, rope_lo=rope_lo, rope_hi=rope_hi, tn=tn),
        grid=(b, t // tm, n // tn),
        in_specs=in_specs,
        out_specs=pl.BlockSpec((None, tm, tn), lambda bi, i, j: (bi, i, j)),
        out_shape=jax.ShapeDtypeStruct((b, t, n), F32),
        scratch_shapes=[pltpu.VMEM((tm, d), BF16)],
        compiler_params=_cparams(("arbitrary", "arbitrary", "arbitrary")),
        name="inproj_rope" if rope else "inproj_ctx",
    )(*args)


def _shift_lerp(p, prev_blk, next_blk, mu, has_prev, has_next):
    rows = p.shape[0]
    ridx = lax.broadcasted_iota(jnp.int32, p.shape, 0)
    prev_row = jnp.where(has_prev, prev_blk[HALO - 1:HALO, :], 0.0)
    next_row = jnp.where(has_next, next_blk[0:1, :], 0.0)
    prev = jnp.where(ridx == 0, prev_row, pltpu.roll(p, 1, 0))
    nxt = jnp.where(ridx == rows - 1, next_row, pltpu.roll(p, rows - 1, 0))
    return p + mu * (0.5 * (prev + nxt) - p)


def _rwkv_kernel(*refs, n_chunks, seq_len):
    (rkv_f, lo_f, rkv_fp, lo_fp, rkv_fn, lo_fn,
     rkv_b, lo_b, rkv_bp, lo_bp, rkv_bn, lo_bn,
     mu_rkv_ref, mu_lo_ref, w0_ref, w2_ref, a0_ref, a2_ref, kk_ref, ka_ref, rk_ref, s0_ref,
     yf_ref, yb_ref, bf_ref, bb_ref, sfin_ref, state) = refs
    c = pl.program_id(1)
    cn = CHUNK

    @pl.when(c == 0)
    def _():
        state[...] = s0_ref[...]

    bd = _block_diag_ones()
    ri = lax.broadcasted_iota(jnp.int32, (cn, cn), 0)
    ci = lax.broadcasted_iota(jnp.int32, (cn, cn), 1)
    qr = lax.broadcasted_iota(jnp.int32, (QW, QW), 0)
    qc = lax.broadcasted_iota(jnp.int32, (QW, QW), 1)
    same_head = (qr // HEAD) == (qc // HEAD)
    eye = (qr == qc).astype(F32)
    lane_head = lax.broadcasted_iota(jnp.int32, (cn, QW), 1) // HEAD
    first_half = lax.broadcasted_iota(jnp.int32, (2 * QW, 2 * cn), 1) < cn
    mu_rkv = mu_rkv_ref[...]
    mu_lo = mu_lo_ref[...]
    k_k = kk_ref[...]
    k_a = ka_ref[...]
    r_k = rk_ref[...]

    dirs = ((0, c, rkv_f, lo_f, rkv_fp, lo_fp, rkv_fn, lo_fn, yf_ref, bf_ref),
            (1, n_chunks - 1 - c, rkv_b, lo_b, rkv_bp, lo_bp, rkv_bn, lo_bn, yb_ref, bb_ref))
    prep = []
    for d, cd, rkv_ref, lo_ref, rkv_p, lo_p, rkv_n, lo_n, y_ref, bon_ref in dirs:
        has_prev = cd > 0
        has_next = cd < n_chunks - 1
        m = _shift_lerp(rkv_ref[...], rkv_p[...], rkv_n[...], mu_rkv, has_prev, has_next)
        ml = _shift_lerp(lo_ref[...], lo_p[...], lo_n[...], mu_lo, has_prev, has_next)
        r = m[:, 0:RW]
        k = m[:, RW:2 * RW]
        v = m[:, 2 * RW:3 * RW]
        wd = ml[:, d * DECAY_LORA:(d + 1) * DECAY_LORA]
        ad = ml[:, 2 * DECAY_LORA + d * AAA_LORA:2 * DECAY_LORA + (d + 1) * AAA_LORA]
        w_pre = w0_ref[d] + _bdot(jnp.tanh(wd), w2_ref[d])
        z = -w_pre
        softplus = jnp.maximum(z, 0.0) + jnp.log(1.0 + jnp.exp(-jnp.abs(z)))
        logw = -jnp.exp(-softplus - 0.5)
        a = jax.nn.sigmoid(a0_ref[d] + _bdot(ad, a2_ref[d]))
        kk = k * k_k
        kk = kk * lax.rsqrt(jnp.maximum(_head_sum(kk * kk, bd), 1e-24))
        k_mod = k * (1.0 + (a - 1.0) * k_a)
        bon_ref[...] = _head_sum(r * k_mod * r_k, bd) * v

        if d == 0:
            tri = jnp.where(ci <= ri, 1.0, 0.0).astype(BF16)
            strict = (qc % cn) < (qr % cn)
            incl = (qc % cn) <= (qr % cn)
            last = cn - 1
        else:
            tri = jnp.where(ci >= ri, 1.0, 0.0).astype(BF16)
            strict = (qc % cn) > (qr % cn)
            incl = (qc % cn) >= (qr % cn)
            last = 0
        cum = sum(jnp.dot(tri, t, preferred_element_type=F32)
                  for t in _bf16_terms(logw, 3))
        e_pos = jnp.exp(cum)
        e_neg = jnp.exp(-cum)
        p_last = e_pos[last:last + 1, :]
        bh = (kk * a) * e_neg
        kh = k_mod * e_neg
        prep.append(dict(d=d, at=(-kk) * jnp.exp(cum - logw), rt=r * e_pos, bh=bh, kh=kh, v=v,
                         bbar=bh * p_last, kbar=kh * p_last, p_last=p_last,
                         m_strict=same_head & strict, m_incl=same_head & incl, y_ref=y_ref))

    def rep(x):
        return jnp.concatenate([x] * QUAD, axis=0)

    def to_bd(x):
        return jnp.where(same_head, rep(x), 0.0)

    def collapse(x):
        out = jnp.where(lane_head == 0, x[0:cn], 0.0)
        for h in range(1, QUAD):
            out = out + jnp.where(lane_head == h, x[h * cn:(h + 1) * cn], 0.0)
        return out

    all_items = []
    for pr in prep:
        for q in range(HEADS // QUAD):
            sl = slice(q * QW, (q + 1) * QW)
            it = {name: pr[name][:, sl] for name in ("at", "rt", "bh", "kh", "v", "bbar", "kbar", "p_last")}
            it.update(d=pr["d"], q=q, m_strict=pr["m_strict"], m_incl=pr["m_incl"])
            all_items.append(it)

    def run_stages(items):
        for it in items:
            g = _bdot_nt(jnp.concatenate([to_bd(it["at"]), to_bd(it["rt"])], axis=0),
                         jnp.concatenate([it["bh"], it["kh"]], axis=0))
            g_sw = pltpu.roll(g, cn, 1)
            gb = jnp.where(first_half, g, g_sw)
            gk = jnp.where(first_half, g_sw, g)
            gb = jnp.concatenate([gb] * (QW // (2 * cn)), axis=1)
            gk = jnp.concatenate([gk] * (QW // (2 * cn)), axis=1)
            it["a_ak"] = jnp.where(it["m_strict"], gk[:QW], 0.0)
            it["a_rb"] = jnp.where(it["m_incl"], gb[QW:], 0.0)
            it["a_rk"] = jnp.where(it["m_incl"], gk[QW:], 0.0)
            it["p"] = jnp.where(it["m_strict"], gb[:QW], 0.0)
            it["t"] = eye + it["p"]
        for it in items:
            pb = it["p"].astype(BF16)
            it["p"] = jnp.dot(pb, pb, preferred_element_type=F32)
        for _ in range(4):
            for it in items:
                tp = _bdot(jnp.concatenate([it["t"], it["p"]], axis=0), it["p"])
                it["t"] = it["t"] + tp[:QW]
                it["p"] = tp[QW:]
        for it in items:
            it["t"] = it["t"] + _bdot(it["t"], it["p"])
        for it in items:
            av = _bdot(jnp.concatenate([it["a_ak"], it["a_rk"]], axis=0), rep(it["v"]))
            it["aakv"] = collapse(av[:QW])
            it["y0"] = collapse(av[QW:])
        for it in items:
            it["xt"] = jnp.concatenate([it["bbar"], it["kbar"]], axis=0).T
            it["pcol"] = jnp.broadcast_to(it["p_last"], (LANES, QW)).T[:, 0:1]
            it["t"] = it["t"].astype(BF16)
        for it in items:
            it["s"] = state[it["d"], it["q"]]
        for it in items:
            zs = _bdot(jnp.concatenate([it["at"], it["rt"]], axis=0), it["s"])
            it["z"] = zs[:cn] + it["aakv"]
            it["ys"] = zs[cn:]
        for it in items:
            it["u"] = collapse(_bdot(it["t"], rep(it["z"])))
        for it in items:
            it["y"] = it["ys"] + collapse(_bdot(it["a_rb"], rep(it["u"]))) + it["y0"]
        for it in items:
            upd = _bdot(it["xt"], jnp.concatenate([it["u"], it["v"]], axis=0))
            state[it["d"], it["q"]] = it["s"] * it["pcol"] + jnp.where(same_head, upd, 0.0)

    for first in range(0, len(all_items), INTERLEAVE):
        run_stages(all_items[first:first + INTERLEAVE])
    for pr in prep:
        pr["y_ref"][...] = jnp.concatenate([it["y"] for it in all_items if it["d"] == pr["d"]], axis=1)

    @pl.when(c == n_chunks - 1)
    def _():
        sfin_ref[...] = state[...]


def _rwkv_scan(px, s0, mu_rkv, mu_lo, w0, w2, a0, a2, k_k, k_a, r_k):
    b, t, _ = px.shape
    cn = CHUNK
    nc = t // cn
    assert t % cn == 0 and cn % HALO == 0
    hb = cn // HALO
    n_halo = t // HALO
    lora_blk = COL_LORA // LORA_W

    def fwd(ci):
        return ci

    def bwd(ci):
        return nc - 1 - ci

    def specs(cmap):
        return [
            pl.BlockSpec((None, cn, 3 * RW), lambda bi, ci: (bi, cmap(ci), 0)),
            pl.BlockSpec((None, cn, LORA_W), lambda bi, ci: (bi, cmap(ci), lora_blk)),
            pl.BlockSpec((None, HALO, 3 * RW), lambda bi, ci: (bi, jnp.maximum(cmap(ci) * hb - 1, 0), 0)),
            pl.BlockSpec((None, HALO, LORA_W),
                         lambda bi, ci: (bi, jnp.maximum(cmap(ci) * hb - 1, 0), lora_blk)),
            pl.BlockSpec((None, HALO, 3 * RW),
                         lambda bi, ci: (bi, jnp.minimum((cmap(ci) + 1) * hb, n_halo - 1), 0)),
            pl.BlockSpec((None, HALO, LORA_W),
                         lambda bi, ci: (bi, jnp.minimum((cmap(ci) + 1) * hb, n_halo - 1), lora_blk)),
        ]

    def const(shape):
        nd = len(shape)
        return pl.BlockSpec(shape, lambda bi, ci: (0,) * nd)

    in_specs = specs(fwd) + specs(bwd) + [
        const((1, 3 * RW)), const((1, LORA_W)),
        const((2, 1, RW)), const((2, DECAY_LORA, RW)), const((2, 1, RW)), const((2, AAA_LORA, RW)),
        const((1, RW)), const((1, RW)), const((1, RW)),
        pl.BlockSpec((None, 2, HEADS // QUAD, QW, QW), lambda bi, ci: (bi, 0, 0, 0, 0)),
    ]
    yshape = jax.ShapeDtypeStruct((b, t, RW), F32)
    out_specs = [
        pl.BlockSpec((None, cn, RW), lambda bi, ci: (bi, ci, 0)),
        pl.BlockSpec((None, cn, RW), lambda bi, ci: (bi, nc - 1 - ci, 0)),
        pl.BlockSpec((None, cn, RW), lambda bi, ci: (bi, ci, 0)),
        pl.BlockSpec((None, cn, RW), lambda bi, ci: (bi, nc - 1 - ci, 0)),
        pl.BlockSpec((None, 2, HEADS // QUAD, QW, QW), lambda bi, ci: (bi, 0, 0, 0, 0)),
    ]
    px_args = [px] * 12
    return pl.pallas_call(
        functools.partial(_rwkv_kernel, n_chunks=nc, seq_len=t),
        grid=(b, nc),
        in_specs=in_specs,
        out_specs=out_specs,
        out_shape=[yshape, yshape, yshape, yshape,
                   jax.ShapeDtypeStruct((b, 2, HEADS // QUAD, QW, QW), F32)],
        scratch_shapes=[pltpu.VMEM((2, HEADS // QUAD, QW, QW), F32)],
        compiler_params=_cparams(("arbitrary", "arbitrary")),
        name="rwkv_scan",
    )(*px_args, mu_rkv, mu_lo, w0, w2.astype(BF16), a0, a2.astype(BF16), k_k, k_a, r_k, s0)


def _relayout_in_cols(a):
    n_lora = 2 * DECAY_LORA + 2 * AAA_LORA + GATE_LORA
    rkv = a[..., :3 * RW]
    lora = a[..., 3 * RW:3 * RW + n_lora]
    att = a[..., 3 * RW + n_lora:]
    pad = jnp.zeros(a.shape[:-1] + (LORA_W - n_lora,), a.dtype)
    return jnp.concatenate([rkv, att, lora, pad], axis=-1)


def _rope_tables(t):
    rows = t // GRID_W
    row = jnp.repeat(jnp.arange(rows, dtype=F32), GRID_W)
    col = jnp.tile(jnp.arange(GRID_W, dtype=F32), rows)
    n_freq = HEAD // 4
    inv = ROPE_THETA ** (-jnp.arange(n_freq, dtype=F32) / n_freq)
    ar = row[:, None] * inv
    ac = col[:, None] * inv
    cos = jnp.concatenate([jnp.cos(ar), jnp.cos(ar), jnp.cos(ac), jnp.cos(ac)], axis=-1)
    sin = jnp.concatenate([-jnp.sin(ar), jnp.sin(ar), -jnp.sin(ac), jnp.sin(ac)], axis=-1)
    reps = LANES // HEAD
    return jnp.tile(cos, (1, reps)), jnp.tile(sin, (1, reps))


def _finish_kernel(yf_ref, yb_ref, bf_ref, bb_ref, lo_ref, lo_p, lo_n, mu_lo_ref, g2_ref, lnw_ref, lnb_ref,
                   o_ref, *, n_tiles):
    i = pl.program_id(1)
    bd = _block_diag_ones()
    y = yf_ref[...] + yb_ref[...]
    mean = _head_sum(y, bd) * (1.0 / HEAD)
    yc = y - mean
    var = _head_sum(yc * yc, bd) * (1.0 / HEAD)
    yn = yc * lax.rsqrt(var + GN_EPS) * lnw_ref[...] + lnb_ref[...]
    ml = _shift_lerp(lo_ref[...], lo_p[...], lo_n[...], mu_lo_ref[...], i > 0, i < n_tiles - 1)
    gate = _bdot(jax.nn.sigmoid(ml), g2_ref[...])
    o_ref[...] = ((yn + bf_ref[...] + bb_ref[...]) * gate).astype(o_ref.dtype)


def _rwkv_finish(yf, yb, bf, bb, px, mu_lo, g2_pad, ln_w, ln_b):
    b, t, _ = yf.shape
    tm = _tile(t, 256)
    nt = t // tm
    hb = tm // HALO
    n_halo = t // HALO
    lora_blk = COL_LORA // LORA_W
    tile = pl.BlockSpec((None, tm, RW), lambda bi, i: (bi, i, 0))
    vec = pl.BlockSpec((1, RW), lambda bi, i: (0, 0))
    return pl.pallas_call(
        functools.partial(_finish_kernel, n_tiles=nt),
        grid=(b, nt),
        in_specs=[tile, tile, tile, tile,
                  pl.BlockSpec((None, tm, LORA_W), lambda bi, i: (bi, i, lora_blk)),
                  pl.BlockSpec((None, HALO, LORA_W), lambda bi, i: (bi, jnp.maximum(i * hb - 1, 0), lora_blk)),
                  pl.BlockSpec((None, HALO, LORA_W),
                               lambda bi, i: (bi, jnp.minimum((i + 1) * hb, n_halo - 1), lora_blk)),
                  pl.BlockSpec((1, LORA_W), lambda bi, i: (0, 0)),
                  pl.BlockSpec((LORA_W, RW), lambda bi, i: (0, 0)),
                  vec, vec],
        out_specs=tile,
        out_shape=jax.ShapeDtypeStruct((b, t, RW), BF16),
        compiler_params=_cparams(("arbitrary", "arbitrary")),
        name="rwkv_finish",
    )(yf, yb, bf, bb, px, px, px, mu_lo, g2_pad, ln_w, ln_b)


def _attn_kernel(q_ref, kp_ref, kc_ref, kn_ref, vp_ref, vc_ref, vn_ref, ck_ref, cv_ref, sink_ref, o_ref,
                 *, n_blocks, n_ctx):
    n = pl.program_id(1)
    q = q_ref[...] * ATT_SCALE
    k_win = jnp.concatenate([kp_ref[...], kc_ref[...], kn_ref[...]], axis=0)
    v_win = jnp.concatenate([vp_ref[...], vc_ref[...], vn_ref[...]], axis=0)
    k_all = jnp.concatenate([ck_ref[...], k_win], axis=0).astype(BF16)
    v_all = jnp.concatenate([cv_ref[...], v_win], axis=0).astype(BF16)
    rows = GROUP * BLOCK
    span = n_ctx + 3 * BLOCK
    qi = lax.broadcasted_iota(jnp.int32, (rows, span), 0) % BLOCK + BLOCK
    kj = lax.broadcasted_iota(jnp.int32, (rows, span), 1) - n_ctx
    k_lo = jnp.where(n == 0, BLOCK, 0)
    k_hi = jnp.where(n == n_blocks - 1, 2 * BLOCK, 3 * BLOCK)
    in_win = (jnp.abs(qi - kj) <= WINDOW) & (kj >= k_lo) & (kj < k_hi)
    ok = (kj < 0) | in_win
    sink = sink_ref[...]
    outs = []
    for g in range(KV_HEADS):
        qg = jnp.concatenate([q[:, (g * GROUP + i) * HEAD:(g * GROUP + i + 1) * HEAD] for i in range(GROUP)],
                             axis=0)
        kg = k_all[:, g * HEAD:(g + 1) * HEAD]
        vg = jnp.concatenate([v_all[:, g * HEAD:(g + 1) * HEAD], jnp.ones((span, HEAD), BF16)], axis=1)
        s = jnp.where(ok, _bdot_nt(qg, kg), -jnp.inf)
        sk = jnp.concatenate([jnp.broadcast_to(sink[:, g * GROUP + i:g * GROUP + i + 1], (BLOCK, 1))
                              for i in range(GROUP)], axis=0)
        m = jnp.maximum(jnp.max(s, axis=-1, keepdims=True), sk)
        p = jnp.exp((s - m).astype(BF16))
        og = jnp.dot(p, vg, preferred_element_type=F32)
        denom = og[:, HEAD:HEAD + 1] + jnp.exp(sk - m)
        og = og[:, :HEAD] / denom
        outs += [og[i * BLOCK:(i + 1) * BLOCK] for i in range(GROUP)]
    o_ref[...] = jnp.concatenate(outs, axis=1).astype(o_ref.dtype)


def _attention(px, pc, sink):
    b, t, _ = px.shape
    n_ctx = pc.shape[1]
    nb = t // BLOCK
    kb = COL_AK // KVW
    vb = COL_AV // KVW

    def blk(col, off):
        return pl.BlockSpec((None, BLOCK, KVW), lambda bi, n: (bi, jnp.clip(n + off, 0, nb - 1), col))

    return pl.pallas_call(
        functools.partial(_attn_kernel, n_blocks=nb, n_ctx=n_ctx),
        grid=(b, nb),
        in_specs=[pl.BlockSpec((None, BLOCK, RW), lambda bi, n: (bi, n, COL_Q // RW)),
                  blk(kb, -1), blk(kb, 0), blk(kb, 1), blk(vb, -1), blk(vb, 0), blk(vb, 1),
                  pl.BlockSpec((None, n_ctx, KVW), lambda bi, n: (bi, 0, kb)),
                  pl.BlockSpec((None, n_ctx, KVW), lambda bi, n: (bi, 0, vb)),
                  pl.BlockSpec((1, HEADS), lambda bi, n: (0, 0))],
        out_specs=pl.BlockSpec((None, BLOCK, RW), lambda bi, n: (bi, n, 0)),
        out_shape=jax.ShapeDtypeStruct((b, t, RW), BF16),
        compiler_params=_cparams(("arbitrary", "arbitrary")),
        name="window_attention",
    )(px, px, px, px, px, px, px, pc, pc, sink.reshape(1, HEADS))


def _ffn_pre(x_new, gf_ref, scf_ref, shf_ref, rw_ref, xh_out, aff_out):
    d = x_new.shape[1]
    xh_out[:, 0:d] = x_new
    hf = _rms_mod(x_new, gf_ref[...], scf_ref[...], shf_ref[...])
    xh_out[:, d:2 * d] = hf
    logits = _bdot(hf, rw_ref[...])
    e = jnp.exp(logits - jnp.max(logits, axis=-1, keepdims=True))
    aff_out[...] = e / jnp.sum(e, axis=-1, keepdims=True)


def _outproj_kernel(ra_ref, at_ref, w_ref, x_ref, gt_ref, gf_ref, scf_ref, shf_ref, rw_ref,
                    xh_out, aff_out, *, n_tiles):
    i = pl.program_id(1)

    @pl.when(i < n_tiles)
    def _():
        y = (jnp.dot(ra_ref[...], w_ref[0:RW, :], preferred_element_type=F32)
             + jnp.dot(at_ref[...], w_ref[RW:2 * RW, :], preferred_element_type=F32))
        _ffn_pre(x_ref[...] + gt_ref[...] * y, gf_ref, scf_ref, shf_ref, rw_ref, xh_out, aff_out)

    @pl.when(i == n_tiles)
    def _():
        xh_out[...] = jnp.zeros(xh_out.shape, F32)


def _mod_specs(d):
    return pl.BlockSpec((None, 1, d), lambda bi, i: (bi, 0, 0))


def _outproj(rwkv_x, att_x, w_out, x, gt_a, g_ffn, sc_f, sh_f, router_w):
    b, t, d = x.shape
    ne = router_w.shape[1]
    tm = _row_tile(t)
    nt = t // tm
    assert _xh_rows(t, ne) == t + tm

    def row(bi, i):
        return (bi, jnp.minimum(i, nt - 1), 0)

    xt = pl.BlockSpec((None, tm, d), row)
    mt = pl.BlockSpec((None, tm, RW), row)
    return pl.pallas_call(
        functools.partial(_outproj_kernel, n_tiles=nt),
        grid=(b, nt + 1),
        in_specs=[mt, mt, pl.BlockSpec((2 * RW, d), lambda bi, i: (0, 0)), xt, _mod_specs(d),
                  pl.BlockSpec((1, d), lambda bi, i: (0, 0)), _mod_specs(d), _mod_specs(d),
                  pl.BlockSpec((d, ne), lambda bi, i: (0, 0))],
        out_specs=[pl.BlockSpec((None, tm, 2 * d), lambda bi, i: (bi, i, 0)),
                   pl.BlockSpec((None, tm, ne), row)],
        out_shape=[jax.ShapeDtypeStruct((b, _xh_rows(t, ne), 2 * d), F32), jax.ShapeDtypeStruct((b, t, ne), F32)],
        compiler_params=_cparams(("arbitrary", "arbitrary")),
        name="outproj_ffnpre",
    )(rwkv_x, att_x, w_out, x, gt_a, g_ffn.reshape(1, d), sc_f, sh_f, router_w)


def _pool_kernel(*refs, tm, seq_len):
    i = pl.program_id(1)
    xh_out = refs[13]

    @pl.when(i < seq_len // tm)
    def _():
        _pool_tile(*refs, tm=tm, seq_len=seq_len)

    @pl.when(i == seq_len // tm)
    def _():
        xh_out[...] = jnp.zeros(xh_out.shape, F32)


def _pool_tile(x_ref, xp_ref, xn_ref, g_ref, sc_ref, sh_ref, pw_ref, ps_ref, gt_ref,
               gf_ref, scf_ref, shf_ref, rw_ref, xh_out, aff_out, ext, *, tm, seq_len):
    i = pl.program_id(1)
    n_tiles = seq_len // tm
    x = x_ref[...]
    d = x.shape[1]
    grp = d // len(POOL_WINDOWS)
    h = _rms_mod(x, g_ref[...], sc_ref[...], sh_ref[...])
    hp = _rms_mod(xp_ref[...], g_ref[...], sc_ref[...], sh_ref[...])
    hn = _rms_mod(xn_ref[...], g_ref[...], sc_ref[...], sh_ref[...])
    ext[0:HALO, :] = jnp.where(i > 0, hp, 0.0)
    ext[HALO:HALO + tm, :] = h
    ext[HALO + tm:2 * HALO + tm, :] = jnp.where(i < n_tiles - 1, hn, 0.0)
    pos = i * tm + lax.broadcasted_iota(jnp.int32, (tm, 1), 0)
    outs = []
    for gi, w in enumerate(POOL_WINDOWS):
        cols = slice(gi * grp, (gi + 1) * grp)
        s = ext[HALO - w // 2:HALO - w // 2 + tm, cols]
        for j in range(-w // 2 + 1, w // 2):
            s = s + ext[HALO + j:HALO + j + tm, cols]
        cnt = (jnp.minimum(pos + w // 2, seq_len) - jnp.maximum(pos - w // 2, 0)).astype(F32)
        outs.append(_bdot(s / cnt - h[:, cols], pw_ref[gi]))
    y = jnp.concatenate(outs, axis=1) * ps_ref[...]
    _ffn_pre(x + gt_ref[...] * y, gf_ref, scf_ref, shf_ref, rw_ref, xh_out, aff_out)


def _pool_layer(x, t, g_mix, sc_a, sh_a, pool_w, pool_scale, gt_a, g_ffn, sc_f, sh_f, router_w):
    b = x.shape[0]
    d = g_mix.shape[0]
    ne = router_w.shape[1]
    ng, grp, _ = pool_w.shape
    assert max(POOL_WINDOWS) // 2 <= HALO
    tm = _row_tile(t)
    nt = t // tm
    assert _xh_rows(t, ne) == t + tm
    hb = tm // HALO
    n_halo = t // HALO

    def row(bi, i):
        return (bi, jnp.minimum(i, nt - 1), 0)

    xt = pl.BlockSpec((None, tm, d), row)
    vec = pl.BlockSpec((1, d), lambda bi, i: (0, 0))
    return pl.pallas_call(
        functools.partial(_pool_kernel, tm=tm, seq_len=t),
        grid=(b, nt + 1),
        in_specs=[xt,
                  pl.BlockSpec((None, HALO, d), lambda bi, i: (bi, jnp.maximum(i * hb - 1, 0), 0)),
                  pl.BlockSpec((None, HALO, d), lambda bi, i: (bi, jnp.minimum((i + 1) * hb, n_halo - 1), 0)),
                  vec, _mod_specs(d), _mod_specs(d),
                  pl.BlockSpec((ng, grp, grp), lambda bi, i: (0, 0, 0)), vec, _mod_specs(d),
                  vec, _mod_specs(d), _mod_specs(d), pl.BlockSpec((d, ne), lambda bi, i: (0, 0))],
        out_specs=[pl.BlockSpec((None, tm, 2 * d), lambda bi, i: (bi, i, 0)),
                   pl.BlockSpec((None, tm, ne), row)],
        out_shape=[jax.ShapeDtypeStruct((b, _xh_rows(t, ne), 2 * d), F32), jax.ShapeDtypeStruct((b, t, ne), F32)],
        scratch_shapes=[pltpu.VMEM((tm + 2 * HALO, d), F32)],
        compiler_params=_cparams(("arbitrary", "arbitrary")),
        name="pool_ffnpre",
    )(x, x, x, g_mix.reshape(1, d), sc_a, sh_a, pool_w.astype(BF16), pool_scale.reshape(1, d), gt_a,
      g_ffn.reshape(1, d), sc_f, sh_f, router_w)


def _route_threshold_kernel(aff_ref, thr_ref, *, cap):
    aff = aff_ref[...]
    ne = aff.shape[0]
    capf = float(cap)

    def bit_step(i, thr):
        cand = thr | (jnp.int32(1) << (30 - i))
        cnt = jnp.sum(jnp.where(aff >= pltpu.bitcast(cand, F32), 1.0, 0.0), axis=(1, 2), keepdims=True)
        return jnp.where(cnt >= capf, cand, thr)

    thr = lax.fori_loop(0, 31, bit_step, jnp.zeros((ne, 1, 1), jnp.int32))
    thr_ref[...] = jnp.broadcast_to(thr, thr_ref.shape)


def _route_kernel(aff_ref, thr_ref, idx_ref, gate_ref, *, cap):
    nr = aff_ref.shape[0]
    aff = aff_ref[...]
    capf = float(cap)
    thr = thr_ref[0:1, 0:1]
    gt = aff >= pltpu.bitcast(thr + 1, F32)
    eq = (aff >= pltpu.bitcast(thr, F32)) & jnp.logical_not(gt)
    need = capf - jnp.sum(jnp.where(gt, 1.0, 0.0), keepdims=True)

    ri = lax.broadcasted_iota(jnp.int32, (LANES, LANES), 0)
    ci = lax.broadcasted_iota(jnp.int32, (LANES, LANES), 1)
    upper = (ri <= ci).astype(BF16)
    rr = lax.broadcasted_iota(jnp.int32, (nr, nr), 0)
    rc = lax.broadcasted_iota(jnp.int32, (nr, nr), 1)
    lower_strict = (rc < rr).astype(BF16)
    ones8 = jnp.ones((8, LANES), BF16)
    p_iota = lax.broadcasted_iota(jnp.int32, (cap, nr), 0).astype(F32)
    r_iota = lax.broadcasted_iota(jnp.int32, (cap, nr), 1).astype(F32)
    c_iota = lax.broadcasted_iota(jnp.int32, (cap, LANES), 1).astype(F32)
    p_col = lax.broadcasted_iota(jnp.int32, (cap, 1), 0).astype(F32)
    upper_r = (rr <= rc).astype(BF16)

    eq_b = jnp.where(eq, 1.0, 0.0).astype(BF16)
    eq_rank = (jnp.dot(eq_b, upper, preferred_element_type=F32) - eq_b.astype(F32)
               + jnp.sum(jnp.dot(lower_strict, eq_b, preferred_element_type=F32), axis=1, keepdims=True))
    sel = gt | (eq & (eq_rank < need))
    sel_b = jnp.where(sel, 1.0, 0.0).astype(BF16)
    lc = jnp.dot(sel_b, upper, preferred_element_type=F32)
    row_tot = lax.dot_general(ones8, sel_b, (((1,), (1,)), ((), ())),
                              preferred_element_type=F32)[0:1, :]
    row_incl = jnp.dot(row_tot.astype(BF16), upper_r, preferred_element_type=F32)
    row_excl = row_incl - row_tot
    hit = (row_excl <= p_iota) & (p_iota < row_incl)
    hit_b = jnp.where(hit, 1.0, 0.0).astype(BF16)
    r_of_p = jnp.sum(jnp.where(hit, r_iota, 0.0), axis=1, keepdims=True)
    base = jnp.sum(jnp.where(hit, row_excl, 0.0), axis=1, keepdims=True)
    lc_row = jnp.dot(hit_b, lc.astype(BF16), preferred_element_type=F32)
    c_of_p = jnp.sum(jnp.where(lc_row <= (p_col - base), 1.0, 0.0), axis=1, keepdims=True)
    idx_ref[...] = (r_of_p * float(LANES) + c_of_p).astype(jnp.int32)
    a1 = aff.astype(BF16)
    a2 = (aff - a1.astype(F32)).astype(BF16)
    a3 = (aff - a1.astype(F32) - a2.astype(F32)).astype(BF16)
    a_row = (jnp.dot(hit_b, a1, preferred_element_type=F32) + jnp.dot(hit_b, a2, preferred_element_type=F32)
             + jnp.dot(hit_b, a3, preferred_element_type=F32))
    gate_ref[...] = jnp.sum(jnp.where(c_iota == c_of_p, a_row, 0.0), axis=1, keepdims=True)


def _route(aff):
    b, t, ne = aff.shape
    cap = _moe_cap(t, ne)
    assert t % LANES == 0
    nr = t // LANES
    aff_t = jnp.swapaxes(aff, 1, 2).reshape(b, ne, nr, LANES)
    thr = pl.pallas_call(
        functools.partial(_route_threshold_kernel, cap=cap),
        grid=(b,),
        in_specs=[pl.BlockSpec((None, ne, nr, LANES), lambda bi: (bi, 0, 0, 0))],
        out_specs=pl.BlockSpec((None, ne, 8, LANES), lambda bi: (bi, 0, 0, 0)),
        out_shape=jax.ShapeDtypeStruct((b, ne, 8, LANES), jnp.int32),
        compiler_params=_cparams(("arbitrary",)),
        name="expert_choice_threshold",
    )(aff_t)
    out_blk = pl.BlockSpec((None, None, cap, 1), lambda bi, e: (bi, e, 0, 0))
    idx, gate = pl.pallas_call(
        functools.partial(_route_kernel, cap=cap),
        grid=(b, ne),
        in_specs=[pl.BlockSpec((None, None, nr, LANES), lambda bi, e: (bi, e, 0, 0)),
                  pl.BlockSpec((None, None, 8, LANES), lambda bi, e: (bi, e, 0, 0))],
        out_specs=[out_blk, out_blk],
        out_shape=[jax.ShapeDtypeStruct((b, ne, cap, 1), jnp.int32),
                   jax.ShapeDtypeStruct((b, ne, cap, 1), F32)],
        compiler_params=_cparams(("arbitrary", "arbitrary")),
        name="expert_choice_route",
    )(aff_t, thr)
    return idx.reshape(b, ne, cap), gate


def _moe_kernel(idx_ref, gate_ref, gt_ref, xh_in, wg_ref, wu_ref, wd_ref, xh_hbm,
                buf, obuf, hbuf, gsem, osem, *, tile, n_exp, n_tiles, n_steps, d, seq_len):
    del xh_in
    bi = pl.program_id(0)
    e = pl.program_id(1)
    ti = pl.program_id(2)
    per_batch = n_exp * n_tiles
    step = bi * per_batch + e * n_tiles + ti
    slot = step % 2

    def row_in(tok, k, b_, slot_):
        return pltpu.make_async_copy(xh_hbm.at[b_, pl.ds(tok, 1)], buf.at[slot_, pl.ds(k, 1)], gsem.at[slot_])

    def row_out(tok, k, b_, slot_):
        return pltpu.make_async_copy(obuf.at[slot_, pl.ds(k, 1)], xh_hbm.at[b_, pl.ds(tok, 1), pl.ds(0, d)],
                                     osem.at[slot_])

    def wait_rows_in(slot_):
        pltpu.make_async_copy(xh_hbm.at[0, pl.ds(0, tile)], buf.at[slot_], gsem.at[slot_]).wait()

    def wait_rows_out(slot_):
        pltpu.make_async_copy(obuf.at[slot_], xh_hbm.at[0, pl.ds(0, tile), pl.ds(0, d)], osem.at[slot_]).wait()

    def fetch_now():
        def body(k, carry):
            row_in(idx_ref[step * tile + k], k, bi, slot).start()
            return carry

        lax.fori_loop(0, tile, body, 0, unroll=8)

    starts_expert = (ti == 0) & (e > 0)
    ends_expert = ((ti == n_tiles - 1) & (e < n_exp - 1)) | (step == n_steps - 1)

    @pl.when(starts_expert)
    def _():
        wait_rows_out(slot)
        wait_rows_out(1 - slot)
        fetch_now()

    @pl.when(step == 0)
    def _():
        obuf[...] = jnp.zeros(obuf.shape, F32)
        fetch_now()

    wait_rows_in(slot)
    hbuf[...] = buf[slot, :, d:2 * d].astype(BF16)

    src_step = jnp.where(ends_expert, step, step + 1)
    src_b = src_step // per_batch
    for k in range(tile):
        row_in(idx_ref[src_step * tile + k], k, src_b, 1 - slot).start()
    prev_written = starts_expert | (step == 0)
    prev_step = jnp.maximum(step - 1, 0)
    prev_b = prev_step // per_batch
    for k in range(tile):
        tok = jnp.where(prev_written, seq_len + k, idx_ref[prev_step * tile + k])
        row_out(tok, k, prev_b, 1 - slot).start()

    h = hbuf[...]
    g = jnp.dot(h, wg_ref[...], preferred_element_type=F32)
    u = jnp.dot(h, wu_ref[...], preferred_element_type=F32)
    hid = (g * jax.nn.sigmoid(g)) * u
    y = jnp.dot(hid.astype(BF16), wd_ref[...], preferred_element_type=F32)
    new_rows = buf[slot, :, 0:d] + y * gate_ref[...] * gt_ref[...]

    @pl.when((step > 0) & jnp.logical_not(starts_expert))
    def _():
        wait_rows_out(slot)

    obuf[slot] = new_rows

    @pl.when(ends_expert)
    def _():
        wait_rows_in(1 - slot)

        def put(k, carry):
            row_out(idx_ref[step * tile + k], k, bi, slot).start()
            return carry

        lax.fori_loop(0, tile, put, 0, unroll=8)

    @pl.when(step == n_steps - 1)
    def _():
        wait_rows_out(slot)
        wait_rows_out(1 - slot)


def _moe(xh, seq_len, idx, gate, gt_f, w_gate, w_up, w_down, layer):
    b, t_pad, d2 = xh.shape
    d = d2 // 2
    _, ne, _, ff = w_gate.shape
    cap = idx.shape[2]
    tile = _moe_tile(cap)
    assert cap == _moe_cap(seq_len, ne) and t_pad == _xh_rows(seq_len, ne)
    nt = cap // tile
    n_steps = b * ne * nt
    grid_spec = pltpu.PrefetchScalarGridSpec(
        num_scalar_prefetch=1,
        grid=(b, ne, nt),
        in_specs=[pl.BlockSpec((None, None, tile, 1), lambda bi, e, ti, idx_r: (bi, e, ti, 0)),
                  pl.BlockSpec((None, 1, d), lambda bi, e, ti, idx_r: (bi, 0, 0)),
                  pl.BlockSpec(memory_space=pl.ANY),
                  pl.BlockSpec((None, None, d, ff), lambda bi, e, ti, idx_r: (layer, e, 0, 0)),
                  pl.BlockSpec((None, None, d, ff), lambda bi, e, ti, idx_r: (layer, e, 0, 0)),
                  pl.BlockSpec((None, None, ff, d), lambda bi, e, ti, idx_r: (layer, e, 0, 0))],
        out_specs=pl.BlockSpec(memory_space=pl.ANY),
        scratch_shapes=[pltpu.VMEM((2, tile, d2), F32), pltpu.VMEM((2, tile, d), F32), pltpu.VMEM((tile, d), BF16),
                        pltpu.SemaphoreType.DMA((2,)), pltpu.SemaphoreType.DMA((2,))],
    )
    return pl.pallas_call(
        functools.partial(_moe_kernel, tile=tile, n_exp=ne, n_tiles=nt, n_steps=n_steps, d=d, seq_len=seq_len),
        grid_spec=grid_spec,
        out_shape=jax.ShapeDtypeStruct((b, t_pad, d2), F32),
        input_output_aliases={3: 0},
        compiler_params=_cparams(("arbitrary", "arbitrary", "arbitrary")),
        name="expert_ffn",
    )(idx.reshape(-1), gate, gt_f, xh, w_gate, w_up, w_down)


def _final_norm_kernel(x_ref, g_ref, o_ref):
    x = x_ref[...]
    o_ref[...] = x * lax.rsqrt(jnp.mean(x * x, axis=-1, keepdims=True) + NORM_EPS) * g_ref[...]


def _final_norm(x, t, g):
    b = x.shape[0]
    d = g.shape[0]
    tm = _tile(t, 512)
    xt = pl.BlockSpec((None, tm, d), lambda bi, i: (bi, i, 0))
    return pl.pallas_call(
        _final_norm_kernel,
        grid=(b, t // tm),
        in_specs=[xt, pl.BlockSpec((1, d), lambda bi, i: (0, 0))],
        out_specs=xt,
        out_shape=jax.ShapeDtypeStruct((b, t, d), F32),
        compiler_params=_cparams(("arbitrary", "arbitrary")),
        name="final_norm",
    )(x, g.reshape(1, d))


def _moe_layer(xh, aff, gt_f, w_gate, w_up, w_down, layer):
    idx, gate = _route(aff)
    return _moe(xh, aff.shape[1], idx, gate, gt_f, w_gate, w_up, w_down, layer)


def kernel(x, c, ctx, c_ctx, ada_w, ada_b, norm_mix, norm_ffn, w_in, shift_mu, decay_w0, decay_w2, iclr_a0,
           iclr_a2, gate_g2, k_k, k_a, r_k, ln_w, ln_b, sink, w_out, pool_w, pool_scale, router_w, exp_w_gate,
           exp_w_up, exp_w_down, norm_final):
    b, t, d = x.shape
    depth = ada_w.shape[0]
    assert depth == 2, "the context stream is only advanced for deeper stacks; not implemented"
    assert b + 1 <= 8
    cc = jnp.concatenate([c, c_ctx[None, :], jnp.zeros((8 - b - 1, d), F32)], axis=0)
    mod = _ada(cc, ada_w, ada_b)

    def mods(l, rows):
        return [mod[l, rows, None, i * d:(i + 1) * d] for i in range(6)]

    cos_t, sin_t = _rope_tables(t)
    wg_b, wu_b, wd_b = exp_w_gate.astype(BF16), exp_w_up.astype(BF16), exp_w_down.astype(BF16)
    for l in range(depth):
        sh_a, sc_a, gt_a, sh_f, sc_f, gt_f = mods(l, slice(0, b))
        if l % 2 == 0:
            e = l // 2
            csh_a, csc_a = [jnp.broadcast_to(m, (b, 1, d)) for m in mods(l, slice(b, b + 1))[:2]]
            w_p = _relayout_in_cols(w_in[e]).astype(BF16)
            mu_p = _relayout_in_cols(jnp.concatenate(
                [shift_mu[e], jnp.zeros((w_in.shape[2] - shift_mu.shape[1],), F32)]))
            mu_rkv = mu_p[None, :3 * RW]
            mu_lo = mu_p[None, COL_LORA:]
            px = _inproj(x, norm_mix[l], sc_a, sh_a, w_p, cos_t, sin_t)
            pc = _inproj(ctx, norm_mix[l], csc_a, csh_a, w_p)
            scan_args = (mu_rkv, mu_lo, decay_w0[e][:, None, :], decay_w2[e], iclr_a0[e][:, None, :], iclr_a2[e],
                         k_k[e][None], k_a[e][None], r_k[e][None])
            s0 = jnp.zeros((b, 2, HEADS // QUAD, QW, QW), F32)
            s_ctx = _rwkv_scan(pc, s0, *scan_args)[4]
            yf, yb, bf, bb, _ = _rwkv_scan(px, s_ctx, *scan_args)
            g2_pad = jnp.zeros((LORA_W, RW), F32).at[LORA_GATE:LORA_GATE + GATE_LORA].set(gate_g2[e]).astype(BF16)
            rwkv_x = _rwkv_finish(yf, yb, bf, bb, px, mu_lo, g2_pad, ln_w[e][None], ln_b[e][None])
            att_x = _attention(px, pc, sink[e])
            xh, aff = _outproj(rwkv_x, att_x, w_out[e].astype(BF16), x, gt_a, norm_ffn[l], sc_f, sh_f,
                               router_w[l].astype(BF16))
        else:
            o = l // 2
            xh, aff = _pool_layer(x, t, norm_mix[l], sc_a, sh_a, pool_w[o], pool_scale[o], gt_a, norm_ffn[l],
                                  sc_f, sh_f, router_w[l].astype(BF16))
        x = _moe_layer(xh, aff, gt_f, wg_b, wu_b, wd_b, l)
    return _final_norm(x, t, norm_final)
```

```python
import functools

import jax
import jax.numpy as jnp
from jax import lax
from jax.experimental import pallas as pl
from jax.experimental.pallas import tpu as pltpu

F32 = jnp.float32
BF16 = jnp.bfloat16

GRID_W = 64
HEADS = 16
HEAD = 64
RW = HEADS * HEAD
DECAY_LORA = 64
AAA_LORA = 64
GATE_LORA = 160
GN_EPS = 64e-5
KV_HEADS = 4
GROUP = HEADS // KV_HEADS
KVW = KV_HEADS * HEAD
ATT_SCALE = HEAD ** -0.5
WINDOW = 128
BLOCK = 128
ROPE_THETA = 10000.0
POOL_WINDOWS = (2, 4, 8, 16)
N_EXPERTS = 16
CAPACITY_FACTOR = 2
NORM_EPS = 1e-6

COL_Q = 3 * RW
COL_AK = 4 * RW
COL_AV = 4 * RW + KVW
COL_LORA = 4 * RW + 2 * KVW
LORA_W = 512
LORA_GATE = 2 * DECAY_LORA + 2 * AAA_LORA
CHUNK = 64
QUAD = 4
QW = QUAD * HEAD
INTERLEAVE = 8
assert CHUNK == HEAD
HALO = 8
LANES = 128
VMEM_LIMIT = 56 * 1024 * 1024


def _cparams(sem):
    return pltpu.CompilerParams(dimension_semantics=sem, vmem_limit_bytes=VMEM_LIMIT)


def _tile(n, pref):
    t = min(n, pref)
    assert n % t == 0, (n, t)
    return t


def _moe_cap(t, ne):
    return max(1, CAPACITY_FACTOR * t // ne)


def _moe_tile(cap):
    return _tile(cap, 256)


def _row_tile(t):
    return _tile(t, 256)


def _xh_rows(t, ne):
    assert _moe_tile(_moe_cap(t, ne)) <= _row_tile(t)
    return t + _row_tile(t)


def _bdot(a, b):
    return jnp.dot(a.astype(BF16), b.astype(BF16), preferred_element_type=F32)


def _bdot_nt(a, b):
    return lax.dot_general(a.astype(BF16), b.astype(BF16), (((1,), (1,)), ((), ())),
                           preferred_element_type=F32)


def _bf16_terms(x, n):
    terms = []
    for _ in range(n):
        t = x.astype(BF16)
        terms.append(t)
        x = x - t.astype(F32)
    return terms


def _rms_mod(x, g, sc, sh):
    y = x * lax.rsqrt(jnp.mean(x * x, axis=-1, keepdims=True) + NORM_EPS)
    return (y * g) * (1.0 + sc) + sh


def _head_sum(x, bd):
    cols = []
    for g in range(x.shape[1] // LANES):
        cols.append(sum(jnp.dot(t, bd, preferred_element_type=F32)
                        for t in _bf16_terms(x[:, g * LANES:(g + 1) * LANES], 2)))
    return jnp.concatenate(cols, axis=1)


def _block_diag_ones():
    r = lax.broadcasted_iota(jnp.int32, (LANES, LANES), 0) // HEAD
    c = lax.broadcasted_iota(jnp.int32, (LANES, LANES), 1) // HEAD
    return jnp.where(r == c, 1.0, 0.0).astype(BF16)


def _ada_kernel(cc_ref, w_ref, b_ref, o_ref):
    cc = cc_ref[...]
    s = cc * jax.nn.sigmoid(cc)
    o_ref[...] = _bdot(s, w_ref[...]) + b_ref[...]


def _ada(cc, ada_w, ada_b):
    n_l, d, n = ada_w.shape
    tn = _tile(n, 1536)
    return pl.pallas_call(
        _ada_kernel,
        grid=(n_l, n // tn),
        in_specs=[pl.BlockSpec((8, d), lambda l, j: (0, 0)),
                  pl.BlockSpec((None, d, tn), lambda l, j: (l, 0, j)),
                  pl.BlockSpec((None, 1, tn), lambda l, j: (l, 0, j))],
        out_specs=pl.BlockSpec((None, 8, tn), lambda l, j: (l, 0, j)),
        out_shape=jax.ShapeDtypeStruct((n_l, 8, n), F32),
        compiler_params=_cparams(("arbitrary", "arbitrary")),
        name="adaln",
    )(cc, ada_w, ada_b.reshape(n_l, 1, n))


def _inproj_kernel(*refs, rope_lo, rope_hi, tn):
    if rope_hi > rope_lo:
        x_ref, g_ref, sc_ref, sh_ref, w_ref, cos_ref, sin_ref, o_ref, h_scr = refs
    else:
        x_ref, g_ref, sc_ref, sh_ref, w_ref, o_ref, h_scr = refs
    j = pl.program_id(2)

    @pl.when(j == 0)
    def _():
        h_scr[...] = _rms_mod(x_ref[...], g_ref[...], sc_ref[...], sh_ref[...]).astype(BF16)

    acc = jnp.dot(h_scr[...], w_ref[...], preferred_element_type=F32)
    if rope_hi > rope_lo:
        def store_with_rope(n_groups):
            cos = cos_ref[...]
            sin = sin_ref[...]
            lane = lax.broadcasted_iota(jnp.int32, cos.shape, 1)
            first = (lane % 32) < 16
            outs = []
            for g in range(n_groups):
                a = acc[:, g * LANES:(g + 1) * LANES]
                partner = jnp.where(first, pltpu.roll(a, LANES - 16, 1), pltpu.roll(a, 16, 1))
                outs.append(a * cos + partner * sin)
            if n_groups < tn // LANES:
                outs.append(acc[:, n_groups * LANES:])
            o_ref[...] = jnp.concatenate(outs, axis=1)

        is_q = (j >= rope_lo) & (j < rope_hi)
        is_k = j == rope_hi

        @pl.when(is_q)
        def _():
            store_with_rope(tn // LANES)

        @pl.when(is_k)
        def _():
            store_with_rope(KVW // LANES)

        @pl.when(jnp.logical_not(is_q | is_k))
        def _():
            o_ref[...] = acc
    else:
        o_ref[...] = acc


def _inproj(x, g, sc, sh, w, cos_t=None, sin_t=None):
    b, t, d = x.shape
    n = w.shape[1]
    tm = _tile(t, 1024)
    tn = 1024
    assert COL_Q % tn == 0 and COL_AK % tn == 0 and KVW <= tn
    rope = cos_t is not None
    rope_lo, rope_hi = (COL_Q // tn, COL_AK // tn) if rope else (0, 0)
    in_specs = [pl.BlockSpec((None, tm, d), lambda bi, i, j: (bi, i, 0)),
                pl.BlockSpec((1, d), lambda bi, i, j: (0, 0)),
                pl.BlockSpec((None, 1, d), lambda bi, i, j: (bi, 0, 0)),
                pl.BlockSpec((None, 1, d), lambda bi, i, j: (bi, 0, 0)),
                pl.BlockSpec((d, tn), lambda bi, i, j: (0, j))]
    args = [x, g.reshape(1, d), sc, sh, w]
    if rope:
        in_specs += [pl.BlockSpec((tm, LANES), lambda bi, i, j: (i, 0)),
                     pl.BlockSpec((tm, LANES), lambda bi, i, j: (i, 0))]
        args += [cos_t, sin_t]
    return pl.pallas_call(
        functools.partial(_inproj_kernel, rope_lo=rope_lo, rope_hi=rope_hi, tn=tn),
        grid=(b, t // tm, n // tn),
        in_specs=in_specs,
        out_specs=pl.BlockSpec((None, tm, tn), lambda bi, i, j: (bi, i, j)),
        out_shape=jax.ShapeDtypeStruct((b, t, n), F32),
        scratch_shapes=[pltpu.VMEM((tm, d), BF16)],
        compiler_params=_cparams(("arbitrary", "arbitrary", "arbitrary")),
        name="inproj_rope" if rope else "inproj_ctx",
    )(*args)


def _shift_lerp(p, prev_blk, next_blk, mu, has_prev, has_next):
    rows = p.shape[0]
    ridx = lax.broadcasted_iota(jnp.int32, p.shape, 0)
    prev_row = jnp.where(has_prev, prev_blk[HALO - 1:HALO, :], 0.0)
    next_row = jnp.where(has_next, next_blk[0:1, :], 0.0)
    prev = jnp.where(ridx == 0, prev_row, pltpu.roll(p, 1, 0))
    nxt = jnp.where(ridx == rows - 1, next_row, pltpu.roll(p, rows - 1, 0))
    return p + mu * (0.5 * (prev + nxt) - p)


def _rwkv_kernel(*refs, n_chunks):
    (rkv_f, lo_f, rkv_fp, lo_fp, rkv_fn, lo_fn,
     rkv_b, lo_b, rkv_bp, lo_bp, rkv_bn, lo_bn,
     mu_rkv_ref, mu_lo_ref, w0_ref, w2_ref, a0_ref, a2_ref, kk_ref, ka_ref, rk_ref, s0_ref,
     yf_ref, yb_ref, bf_ref, bb_ref, sfin_ref, state) = refs
    c = pl.program_id(1)
    cn = CHUNK

    @pl.when(c == 0)
    def _():
        state[...] = s0_ref[...]

    bd = _block_diag_ones()
    ri = lax.broadcasted_iota(jnp.int32, (cn, cn), 0)
    ci = lax.broadcasted_iota(jnp.int32, (cn, cn), 1)
    qr = lax.broadcasted_iota(jnp.int32, (QW, QW), 0)
    qc = lax.broadcasted_iota(jnp.int32, (QW, QW), 1)
    same_head = (qr // HEAD) == (qc // HEAD)
    eye = (qr == qc).astype(F32)
    lane_head = lax.broadcasted_iota(jnp.int32, (cn, QW), 1) // HEAD
    first_half = lax.broadcasted_iota(jnp.int32, (2 * QW, 2 * cn), 1) < cn
    mu_rkv = mu_rkv_ref[...]
    mu_lo = mu_lo_ref[...]
    k_k = kk_ref[...]
    k_a = ka_ref[...]
    r_k = rk_ref[...]

    dirs = ((0, c, rkv_f, lo_f, rkv_fp, lo_fp, rkv_fn, lo_fn, yf_ref, bf_ref),
            (1, n_chunks - 1 - c, rkv_b, lo_b, rkv_bp, lo_bp, rkv_bn, lo_bn, yb_ref, bb_ref))
    prep = []
    for d, cd, rkv_ref, lo_ref, rkv_p, lo_p, rkv_n, lo_n, y_ref, bon_ref in dirs:
        has_prev = cd > 0
        has_next = cd < n_chunks - 1
        m = _shift_lerp(rkv_ref[...], rkv_p[...], rkv_n[...], mu_rkv, has_prev, has_next)
        ml = _shift_lerp(lo_ref[...], lo_p[...], lo_n[...], mu_lo, has_prev, has_next)
        r = m[:, 0:RW]
        k = m[:, RW:2 * RW]
        v = m[:, 2 * RW:3 * RW]
        wd = ml[:, d * DECAY_LORA:(d + 1) * DECAY_LORA]
        ad = ml[:, 2 * DECAY_LORA + d * AAA_LORA:2 * DECAY_LORA + (d + 1) * AAA_LORA]
        w_pre = w0_ref[d] + _bdot(jnp.tanh(wd), w2_ref[d])
        z = -w_pre
        softplus = jnp.maximum(z, 0.0) + jnp.log(1.0 + jnp.exp(-jnp.abs(z)))
        logw = -jnp.exp(-softplus - 0.5)
        a = jax.nn.sigmoid(a0_ref[d] + _bdot(ad, a2_ref[d]))
        kk = k * k_k
        kk = kk * lax.rsqrt(jnp.maximum(_head_sum(kk * kk, bd), 1e-24))
        k_mod = k * (1.0 + (a - 1.0) * k_a)
        bon_ref[...] = _head_sum(r * k_mod * r_k, bd) * v

        if d == 0:
            tri = jnp.where(ci <= ri, 1.0, 0.0).astype(BF16)
            strict = (qc % cn) < (qr % cn)
            incl = (qc % cn) <= (qr % cn)
            last = cn - 1
        else:
            tri = jnp.where(ci >= ri, 1.0, 0.0).astype(BF16)
            strict = (qc % cn) > (qr % cn)
            incl = (qc % cn) >= (qr % cn)
            last = 0
        cum = sum(jnp.dot(tri, t, preferred_element_type=F32)
                  for t in _bf16_terms(logw, 3))
        e_pos = jnp.exp(cum)
        e_neg = jnp.exp(-cum)
        p_last = e_pos[last:last + 1, :]
        bh = (kk * a) * e_neg
        kh = k_mod * e_neg
        prep.append(dict(d=d, at=(-kk) * jnp.exp(cum - logw), rt=r * e_pos, bh=bh, kh=kh, v=v,
                         bbar=bh * p_last, kbar=kh * p_last, p_last=p_last,
                         m_strict=same_head & strict, m_incl=same_head & incl, y_ref=y_ref))

    def rep(x):
        return jnp.concatenate([x] * QUAD, axis=0)

    def to_bd(x):
        return jnp.where(same_head, rep(x), 0.0)

    def collapse(x):
        out = jnp.where(lane_head == 0, x[0:cn], 0.0)
        for h in range(1, QUAD):
            out = out + jnp.where(lane_head == h, x[h * cn:(h + 1) * cn], 0.0)
        return out

    all_items = []
    for pr in prep:
        for q in range(HEADS // QUAD):
            sl = slice(q * QW, (q + 1) * QW)
            it = {name: pr[name][:, sl] for name in ("at", "rt", "bh", "kh", "v", "bbar", "kbar", "p_last")}
            it.update(d=pr["d"], q=q, m_strict=pr["m_strict"], m_incl=pr["m_incl"])
            all_items.append(it)

    def run_stages(items):
        for it in items:
            g = _bdot_nt(jnp.concatenate([to_bd(it["at"]), to_bd(it["rt"])], axis=0),
                         jnp.concatenate([it["bh"], it["kh"]], axis=0))
            g_sw = pltpu.roll(g, cn, 1)
            gb = jnp.where(first_half, g, g_sw)
            gk = jnp.where(first_half, g_sw, g)
            gb = jnp.concatenate([gb] * (QW // (2 * cn)), axis=1)
            gk = jnp.concatenate([gk] * (QW // (2 * cn)), axis=1)
            it["a_ak"] = jnp.where(it["m_strict"], gk[:QW], 0.0)
            it["a_rb"] = jnp.where(it["m_incl"], gb[QW:], 0.0)
            it["a_rk"] = jnp.where(it["m_incl"], gk[QW:], 0.0)
            it["p"] = jnp.where(it["m_strict"], gb[:QW], 0.0)
            it["t"] = eye + it["p"]
        for it in items:
            pb = it["p"].astype(BF16)
            it["p"] = jnp.dot(pb, pb, preferred_element_type=F32)
        for _ in range(4):
            for it in items:
                pb = it["p"].astype(BF16)
                tp = jnp.dot(jnp.concatenate([it["t"].astype(BF16), pb], axis=0), pb,
                             preferred_element_type=F32)
                it["t"] = it["t"] + tp[:QW]
                it["p"] = tp[QW:]
        for it in items:
            it["t"] = it["t"] + _bdot(it["t"], it["p"])
        for it in items:
            av = _bdot(jnp.concatenate([it["a_ak"], it["a_rk"]], axis=0), rep(it["v"].astype(BF16)))
            it["aakv"] = collapse(av[:QW])
            it["y0"] = collapse(av[QW:])
        for it in items:
            it["xt"] = jnp.concatenate([it["bbar"], it["kbar"]], axis=0).T
            it["pcol"] = jnp.broadcast_to(it["p_last"], (LANES, QW)).T[:, 0:1]
            it["t"] = it["t"].astype(BF16)
        for it in items:
            it["s"] = state[it["d"], it["q"]]
        for it in items:
            zs = _bdot(jnp.concatenate([it["at"], it["rt"]], axis=0), it["s"])
            it["z"] = zs[:cn] + it["aakv"]
            it["ys"] = zs[cn:]
        for it in items:
            it["u"] = collapse(_bdot(it["t"], rep(it["z"].astype(BF16))))
        for it in items:
            it["y"] = it["ys"] + collapse(_bdot(it["a_rb"], rep(it["u"].astype(BF16)))) + it["y0"]
        for it in items:
            upd = _bdot(it["xt"], jnp.concatenate([it["u"], it["v"]], axis=0))
            state[it["d"], it["q"]] = it["s"] * it["pcol"] + jnp.where(same_head, upd, 0.0)

    for first in range(0, len(all_items), INTERLEAVE):
        run_stages(all_items[first:first + INTERLEAVE])
    for pr in prep:
        pr["y_ref"][...] = jnp.concatenate([it["y"] for it in all_items if it["d"] == pr["d"]], axis=1)

    @pl.when(c == n_chunks - 1)
    def _():
        sfin_ref[...] = state[...]


def _rwkv_scan(px, s0, mu_rkv, mu_lo, w0, w2, a0, a2, k_k, k_a, r_k):
    b, t, _ = px.shape
    cn = CHUNK
    nc = t // cn
    assert t % cn == 0 and cn % HALO == 0
    hb = cn // HALO
    n_halo = t // HALO
    lora_blk = COL_LORA // LORA_W

    def fwd(ci):
        return ci

    def bwd(ci):
        return nc - 1 - ci

    def specs(cmap):
        return [
            pl.BlockSpec((None, cn, 3 * RW), lambda bi, ci: (bi, cmap(ci), 0)),
            pl.BlockSpec((None, cn, LORA_W), lambda bi, ci: (bi, cmap(ci), lora_blk)),
            pl.BlockSpec((None, HALO, 3 * RW), lambda bi, ci: (bi, jnp.maximum(cmap(ci) * hb - 1, 0), 0)),
            pl.BlockSpec((None, HALO, LORA_W),
                         lambda bi, ci: (bi, jnp.maximum(cmap(ci) * hb - 1, 0), lora_blk)),
            pl.BlockSpec((None, HALO, 3 * RW),
                         lambda bi, ci: (bi, jnp.minimum((cmap(ci) + 1) * hb, n_halo - 1), 0)),
            pl.BlockSpec((None, HALO, LORA_W),
                         lambda bi, ci: (bi, jnp.minimum((cmap(ci) + 1) * hb, n_halo - 1), lora_blk)),
        ]

    def const(shape):
        nd = len(shape)
        return pl.BlockSpec(shape, lambda bi, ci: (0,) * nd)

    in_specs = specs(fwd) + specs(bwd) + [
        const((1, 3 * RW)), const((1, LORA_W)),
        const((2, 1, RW)), const((2, DECAY_LORA, RW)), const((2, 1, RW)), const((2, AAA_LORA, RW)),
        const((1, RW)), const((1, RW)), const((1, RW)),
        pl.BlockSpec((None, 2, HEADS // QUAD, QW, QW), lambda bi, ci: (bi, 0, 0, 0, 0)),
    ]
    yshape = jax.ShapeDtypeStruct((b, t, RW), F32)
    out_specs = [
        pl.BlockSpec((None, cn, RW), lambda bi, ci: (bi, ci, 0)),
        pl.BlockSpec((None, cn, RW), lambda bi, ci: (bi, nc - 1 - ci, 0)),
        pl.BlockSpec((None, cn, RW), lambda bi, ci: (bi, ci, 0)),
        pl.BlockSpec((None, cn, RW), lambda bi, ci: (bi, nc - 1 - ci, 0)),
        pl.BlockSpec((None, 2, HEADS // QUAD, QW, QW), lambda bi, ci: (bi, 0, 0, 0, 0)),
    ]
    px_args = [px] * 12
    return pl.pallas_call(
        functools.partial(_rwkv_kernel, n_chunks=nc),
        grid=(b, nc),
        in_specs=in_specs,
        out_specs=out_specs,
        out_shape=[yshape, yshape, yshape, yshape,
                   jax.ShapeDtypeStruct((b, 2, HEADS // QUAD, QW, QW), F32)],
        scratch_shapes=[pltpu.VMEM((2, HEADS // QUAD, QW, QW), F32)],
        compiler_params=_cparams(("arbitrary", "arbitrary")),
        name="rwkv_scan",
    )(*px_args, mu_rkv, mu_lo, w0, w2.astype(BF16), a0, a2.astype(BF16), k_k, k_a, r_k, s0)


def _relayout_in_cols(a):
    n_lora = 2 * DECAY_LORA + 2 * AAA_LORA + GATE_LORA
    rkv = a[..., :3 * RW]
    lora = a[..., 3 * RW:3 * RW + n_lora]
    att = a[..., 3 * RW + n_lora:]
    pad = jnp.zeros(a.shape[:-1] + (LORA_W - n_lora,), a.dtype)
    return jnp.concatenate([rkv, att, lora, pad], axis=-1)


def _rope_tables(t):
    rows = t // GRID_W
    row = jnp.repeat(jnp.arange(rows, dtype=F32), GRID_W)
    col = jnp.tile(jnp.arange(GRID_W, dtype=F32), rows)
    n_freq = HEAD // 4
    inv = ROPE_THETA ** (-jnp.arange(n_freq, dtype=F32) / n_freq)
    ar = row[:, None] * inv
    ac = col[:, None] * inv
    cos = jnp.concatenate([jnp.cos(ar), jnp.cos(ar), jnp.cos(ac), jnp.cos(ac)], axis=-1)
    sin = jnp.concatenate([-jnp.sin(ar), jnp.sin(ar), -jnp.sin(ac), jnp.sin(ac)], axis=-1)
    reps = LANES // HEAD
    return jnp.tile(cos, (1, reps)), jnp.tile(sin, (1, reps))


def _finish_kernel(yf_ref, yb_ref, bf_ref, bb_ref, lo_ref, lo_p, lo_n, mu_lo_ref, g2_ref, lnw_ref, lnb_ref,
                   o_ref, *, n_tiles):
    i = pl.program_id(1)
    bd = _block_diag_ones()
    y = yf_ref[...] + yb_ref[...]
    mean = _head_sum(y, bd) * (1.0 / HEAD)
    yc = y - mean
    var = _head_sum(yc * yc, bd) * (1.0 / HEAD)
    yn = yc * lax.rsqrt(var + GN_EPS) * lnw_ref[...] + lnb_ref[...]
    ml = _shift_lerp(lo_ref[...], lo_p[...], lo_n[...], mu_lo_ref[...], i > 0, i < n_tiles - 1)
    gate = _bdot(jax.nn.sigmoid(ml), g2_ref[...])
    o_ref[...] = ((yn + bf_ref[...] + bb_ref[...]) * gate).astype(o_ref.dtype)


def _rwkv_finish(yf, yb, bf, bb, px, mu_lo, g2_pad, ln_w, ln_b):
    b, t, _ = yf.shape
    tm = _tile(t, 512)
    nt = t // tm
    hb = tm // HALO
    n_halo = t // HALO
    lora_blk = COL_LORA // LORA_W
    tile = pl.BlockSpec((None, tm, RW), lambda bi, i: (bi, i, 0))
    vec = pl.BlockSpec((1, RW), lambda bi, i: (0, 0))
    return pl.pallas_call(
        functools.partial(_finish_kernel, n_tiles=nt),
        grid=(b, nt),
        in_specs=[tile, tile, tile, tile,
                  pl.BlockSpec((None, tm, LORA_W), lambda bi, i: (bi, i, lora_blk)),
                  pl.BlockSpec((None, HALO, LORA_W), lambda bi, i: (bi, jnp.maximum(i * hb - 1, 0), lora_blk)),
                  pl.BlockSpec((None, HALO, LORA_W),
                               lambda bi, i: (bi, jnp.minimum((i + 1) * hb, n_halo - 1), lora_blk)),
                  pl.BlockSpec((1, LORA_W), lambda bi, i: (0, 0)),
                  pl.BlockSpec((LORA_W, RW), lambda bi, i: (0, 0)),
                  vec, vec],
        out_specs=tile,
        out_shape=jax.ShapeDtypeStruct((b, t, RW), BF16),
        compiler_params=_cparams(("arbitrary", "arbitrary")),
        name="rwkv_finish",
    )(yf, yb, bf, bb, px, px, px, mu_lo, g2_pad, ln_w, ln_b)


def _attn_kernel(q_ref, kp_ref, kc_ref, kn_ref, vp_ref, vc_ref, vn_ref, ck_ref, cv_ref, sink_ref, o_ref,
                 *, n_blocks, n_ctx):
    n = pl.program_id(1)
    q = q_ref[...] * ATT_SCALE
    k_win = jnp.concatenate([kp_ref[...], kc_ref[...], kn_ref[...]], axis=0)
    v_win = jnp.concatenate([vp_ref[...], vc_ref[...], vn_ref[...]], axis=0)
    k_all = jnp.concatenate([ck_ref[...], k_win], axis=0).astype(BF16)
    v_all = jnp.concatenate([cv_ref[...], v_win], axis=0).astype(BF16)
    rows = GROUP * BLOCK
    span = n_ctx + 3 * BLOCK
    qi = lax.broadcasted_iota(jnp.int32, (rows, span), 0) % BLOCK + BLOCK
    kj = lax.broadcasted_iota(jnp.int32, (rows, span), 1) - n_ctx
    k_lo = jnp.where(n == 0, BLOCK, 0)
    k_hi = jnp.where(n == n_blocks - 1, 2 * BLOCK, 3 * BLOCK)
    in_win = (jnp.abs(qi - kj) <= WINDOW) & (kj >= k_lo) & (kj < k_hi)
    ok = (kj < 0) | in_win
    sink = sink_ref[...]
    groups = range(KV_HEADS)
    s_all, sk_all, m_all, p_all, outs = [], [], [], [], []
    for g in groups:
        qg = jnp.concatenate([q[:, (g * GROUP + i) * HEAD:(g * GROUP + i + 1) * HEAD] for i in range(GROUP)],
                             axis=0)
        s_all.append(jnp.where(ok, _bdot_nt(qg, k_all[:, g * HEAD:(g + 1) * HEAD]), -jnp.inf))
        sk_all.append(jnp.concatenate([jnp.broadcast_to(sink[:, g * GROUP + i:g * GROUP + i + 1], (BLOCK, 1))
                                       for i in range(GROUP)], axis=0))
    for g in groups:
        m_all.append(jnp.maximum(jnp.max(s_all[g], axis=-1, keepdims=True), sk_all[g]))
    for g in groups:
        p_all.append(jnp.exp((s_all[g] - m_all[g]).astype(BF16)))
    for g in groups:
        vg = jnp.concatenate([v_all[:, g * HEAD:(g + 1) * HEAD], jnp.ones((span, HEAD), BF16)], axis=1)
        og = jnp.dot(p_all[g], vg, preferred_element_type=F32)
        denom = og[:, HEAD:HEAD + 1] + jnp.exp(sk_all[g] - m_all[g])
        og = og[:, :HEAD] / denom
        outs += [og[i * BLOCK:(i + 1) * BLOCK] for i in range(GROUP)]
    o_ref[...] = jnp.concatenate(outs, axis=1).astype(o_ref.dtype)


def _attention(px, pc, sink):
    b, t, _ = px.shape
    n_ctx = pc.shape[1]
    nb = t // BLOCK
    kb = COL_AK // KVW
    vb = COL_AV // KVW

    def blk(col, off):
        return pl.BlockSpec((None, BLOCK, KVW), lambda bi, n: (bi, jnp.clip(n + off, 0, nb - 1), col))

    return pl.pallas_call(
        functools.partial(_attn_kernel, n_blocks=nb, n_ctx=n_ctx),
        grid=(b, nb),
        in_specs=[pl.BlockSpec((None, BLOCK, RW), lambda bi, n: (bi, n, COL_Q // RW)),
                  blk(kb, -1), blk(kb, 0), blk(kb, 1), blk(vb, -1), blk(vb, 0), blk(vb, 1),
                  pl.BlockSpec((None, n_ctx, KVW), lambda bi, n: (bi, 0, kb)),
                  pl.BlockSpec((None, n_ctx, KVW), lambda bi, n: (bi, 0, vb)),
                  pl.BlockSpec((1, HEADS), lambda bi, n: (0, 0))],
        out_specs=pl.BlockSpec((None, BLOCK, RW), lambda bi, n: (bi, n, 0)),
        out_shape=jax.ShapeDtypeStruct((b, t, RW), BF16),
        compiler_params=_cparams(("arbitrary", "arbitrary")),
        name="window_attention",
    )(px, px, px, px, px, px, px, pc, pc, sink.reshape(1, HEADS))


def _ffn_pre(x_new, gf_ref, scf_ref, shf_ref, rw_ref, xh_out, aff_out):
    d = x_new.shape[1]
    xh_out[:, 0:d] = x_new
    hf = _rms_mod(x_new, gf_ref[...], scf_ref[...], shf_ref[...])
    xh_out[:, d:2 * d] = hf
    logits = _bdot(hf, rw_ref[...])
    e = jnp.exp(logits - jnp.max(logits, axis=-1, keepdims=True))
    aff_out[...] = e / jnp.sum(e, axis=-1, keepdims=True)


def _outproj_kernel(ra_ref, at_ref, w_ref, x_ref, gt_ref, gf_ref, scf_ref, shf_ref, rw_ref,
                    xh_out, aff_out, *, n_tiles):
    i = pl.program_id(1)

    @pl.when(i < n_tiles)
    def _():
        y = (jnp.dot(ra_ref[...], w_ref[0:RW, :], preferred_element_type=F32)
             + jnp.dot(at_ref[...], w_ref[RW:2 * RW, :], preferred_element_type=F32))
        _ffn_pre(x_ref[...] + gt_ref[...] * y, gf_ref, scf_ref, shf_ref, rw_ref, xh_out, aff_out)

    @pl.when(i == n_tiles)
    def _():
        xh_out[...] = jnp.zeros(xh_out.shape, F32)


def _mod_specs(d):
    return pl.BlockSpec((None, 1, d), lambda bi, i: (bi, 0, 0))


def _outproj(rwkv_x, att_x, w_out, x, gt_a, g_ffn, sc_f, sh_f, router_w):
    b, t, d = x.shape
    ne = router_w.shape[1]
    tm = _row_tile(t)
    nt = t // tm
    assert _xh_rows(t, ne) == t + tm

    def row(bi, i):
        return (bi, jnp.minimum(i, nt - 1), 0)

    xt = pl.BlockSpec((None, tm, d), row)
    mt = pl.BlockSpec((None, tm, RW), row)
    return pl.pallas_call(
        functools.partial(_outproj_kernel, n_tiles=nt),
        grid=(b, nt + 1),
        in_specs=[mt, mt, pl.BlockSpec((2 * RW, d), lambda bi, i: (0, 0)), xt, _mod_specs(d),
                  pl.BlockSpec((1, d), lambda bi, i: (0, 0)), _mod_specs(d), _mod_specs(d),
                  pl.BlockSpec((d, ne), lambda bi, i: (0, 0))],
        out_specs=[pl.BlockSpec((None, tm, 2 * d), lambda bi, i: (bi, i, 0)),
                   pl.BlockSpec((None, tm, ne), row)],
        out_shape=[jax.ShapeDtypeStruct((b, _xh_rows(t, ne), 2 * d), F32), jax.ShapeDtypeStruct((b, t, ne), F32)],
        compiler_params=_cparams(("arbitrary", "arbitrary")),
        name="outproj_ffnpre",
    )(rwkv_x, att_x, w_out, x, gt_a, g_ffn.reshape(1, d), sc_f, sh_f, router_w)


def _pool_kernel(*refs, tm, seq_len):
    i = pl.program_id(1)
    xh_out = refs[13]

    @pl.when(i < seq_len // tm)
    def _():
        _pool_tile(*refs, tm=tm, seq_len=seq_len)

    @pl.when(i == seq_len // tm)
    def _():
        xh_out[...] = jnp.zeros(xh_out.shape, F32)


def _pool_tile(x_ref, xp_ref, xn_ref, g_ref, sc_ref, sh_ref, pw_ref, ps_ref, gt_ref,
               gf_ref, scf_ref, shf_ref, rw_ref, xh_out, aff_out, ext, *, tm, seq_len):
    i = pl.program_id(1)
    n_tiles = seq_len // tm
    x = x_ref[...]
    d = x.shape[1]
    grp = d // len(POOL_WINDOWS)
    h = _rms_mod(x, g_ref[...], sc_ref[...], sh_ref[...])
    hp = _rms_mod(xp_ref[...], g_ref[...], sc_ref[...], sh_ref[...])
    hn = _rms_mod(xn_ref[...], g_ref[...], sc_ref[...], sh_ref[...])
    ext[0:HALO, :] = jnp.where(i > 0, hp, 0.0)
    ext[HALO:HALO + tm, :] = h
    ext[HALO + tm:2 * HALO + tm, :] = jnp.where(i < n_tiles - 1, hn, 0.0)
    pos = i * tm + lax.broadcasted_iota(jnp.int32, (tm, 1), 0)
    outs = []
    for gi, w in enumerate(POOL_WINDOWS):
        cols = slice(gi * grp, (gi + 1) * grp)
        s = ext[HALO - w // 2:HALO - w // 2 + tm, cols]
        for j in range(-w // 2 + 1, w // 2):
            s = s + ext[HALO + j:HALO + j + tm, cols]
        cnt = (jnp.minimum(pos + w // 2, seq_len) - jnp.maximum(pos - w // 2, 0)).astype(F32)
        outs.append(_bdot(s / cnt - h[:, cols], pw_ref[gi]))
    y = jnp.concatenate(outs, axis=1) * ps_ref[...]
    _ffn_pre(x + gt_ref[...] * y, gf_ref, scf_ref, shf_ref, rw_ref, xh_out, aff_out)


def _pool_layer(x, t, g_mix, sc_a, sh_a, pool_w, pool_scale, gt_a, g_ffn, sc_f, sh_f, router_w):
    b = x.shape[0]
    d = g_mix.shape[0]
    ne = router_w.shape[1]
    ng, grp, _ = pool_w.shape
    assert max(POOL_WINDOWS) // 2 <= HALO
    tm = _row_tile(t)
    nt = t // tm
    assert _xh_rows(t, ne) == t + tm
    hb = tm // HALO
    n_halo = t // HALO

    def row(bi, i):
        return (bi, jnp.minimum(i, nt - 1), 0)

    xt = pl.BlockSpec((None, tm, d), row)
    vec = pl.BlockSpec((1, d), lambda bi, i: (0, 0))
    return pl.pallas_call(
        functools.partial(_pool_kernel, tm=tm, seq_len=t),
        grid=(b, nt + 1),
        in_specs=[xt,
                  pl.BlockSpec((None, HALO, d), lambda bi, i: (bi, jnp.maximum(i * hb - 1, 0), 0)),
                  pl.BlockSpec((None, HALO, d), lambda bi, i: (bi, jnp.minimum((i + 1) * hb, n_halo - 1), 0)),
                  vec, _mod_specs(d), _mod_specs(d),
                  pl.BlockSpec((ng, grp, grp), lambda bi, i: (0, 0, 0)), vec, _mod_specs(d),
                  vec, _mod_specs(d), _mod_specs(d), pl.BlockSpec((d, ne), lambda bi, i: (0, 0))],
        out_specs=[pl.BlockSpec((None, tm, 2 * d), lambda bi, i: (bi, i, 0)),
                   pl.BlockSpec((None, tm, ne), row)],
        out_shape=[jax.ShapeDtypeStruct((b, _xh_rows(t, ne), 2 * d), F32), jax.ShapeDtypeStruct((b, t, ne), F32)],
        scratch_shapes=[pltpu.VMEM((tm + 2 * HALO, d), F32)],
        compiler_params=_cparams(("arbitrary", "arbitrary")),
        name="pool_ffnpre",
    )(x, x, x, g_mix.reshape(1, d), sc_a, sh_a, pool_w.astype(BF16), pool_scale.reshape(1, d), gt_a,
      g_ffn.reshape(1, d), sc_f, sh_f, router_w)


def _route_threshold_kernel(aff_ref, thr_ref, *, cap):
    aff = aff_ref[...]
    ne = aff.shape[0]
    capf = float(cap)

    def bit_step(i, thr):
        cand = thr | (jnp.int32(1) << (30 - i))
        cnt = jnp.sum(jnp.where(aff >= pltpu.bitcast(cand, F32), 1.0, 0.0), axis=(1, 2), keepdims=True)
        return jnp.where(cnt >= capf, cand, thr)

    thr = lax.fori_loop(0, 31, bit_step, jnp.zeros((ne, 1, 1), jnp.int32))
    thr_ref[...] = jnp.broadcast_to(thr, thr_ref.shape)


def _route_kernel(aff_ref, thr_ref, idx_ref, gate_ref, *, cap):
    nr = aff_ref.shape[0]
    aff = aff_ref[...]
    capf = float(cap)
    thr = thr_ref[0:1, 0:1]
    gt = aff >= pltpu.bitcast(thr + 1, F32)
    eq = (aff >= pltpu.bitcast(thr, F32)) & jnp.logical_not(gt)
    need = capf - jnp.sum(jnp.where(gt, 1.0, 0.0), keepdims=True)

    ri = lax.broadcasted_iota(jnp.int32, (LANES, LANES), 0)
    ci = lax.broadcasted_iota(jnp.int32, (LANES, LANES), 1)
    upper = (ri <= ci).astype(BF16)
    rr = lax.broadcasted_iota(jnp.int32, (nr, nr), 0)
    rc = lax.broadcasted_iota(jnp.int32, (nr, nr), 1)
    lower_strict = (rc < rr).astype(BF16)
    ones8 = jnp.ones((8, LANES), BF16)
    p_iota = lax.broadcasted_iota(jnp.int32, (cap, nr), 0).astype(F32)
    r_iota = lax.broadcasted_iota(jnp.int32, (cap, nr), 1).astype(F32)
    c_iota = lax.broadcasted_iota(jnp.int32, (cap, LANES), 1).astype(F32)
    p_col = lax.broadcasted_iota(jnp.int32, (cap, 1), 0).astype(F32)
    upper_r = (rr <= rc).astype(BF16)

    eq_b = jnp.where(eq, 1.0, 0.0).astype(BF16)
    eq_rank = (jnp.dot(eq_b, upper, preferred_element_type=F32) - eq_b.astype(F32)
               + jnp.sum(jnp.dot(lower_strict, eq_b, preferred_element_type=F32), axis=1, keepdims=True))
    sel = gt | (eq & (eq_rank < need))
    sel_b = jnp.where(sel, 1.0, 0.0).astype(BF16)
    lc = jnp.dot(sel_b, upper, preferred_element_type=F32)
    row_tot = lax.dot_general(ones8, sel_b, (((1,), (1,)), ((), ())),
                              preferred_element_type=F32)[0:1, :]
    row_incl = jnp.dot(row_tot.astype(BF16), upper_r, preferred_element_type=F32)
    row_excl = row_incl - row_tot
    hit = (row_excl <= p_iota) & (p_iota < row_incl)
    hit_b = jnp.where(hit, 1.0, 0.0).astype(BF16)
    r_of_p = jnp.sum(jnp.where(hit, r_iota, 0.0), axis=1, keepdims=True)
    base = jnp.sum(jnp.where(hit, row_excl, 0.0), axis=1, keepdims=True)
    lc_row = jnp.dot(hit_b, lc.astype(BF16), preferred_element_type=F32)
    c_of_p = jnp.sum(jnp.where(lc_row <= (p_col - base), 1.0, 0.0), axis=1, keepdims=True)
    idx_ref[...] = (r_of_p * float(LANES) + c_of_p).astype(jnp.int32)
    a1 = aff.astype(BF16)
    a2 = (aff - a1.astype(F32)).astype(BF16)
    a3 = (aff - a1.astype(F32) - a2.astype(F32)).astype(BF16)
    a_row = (jnp.dot(hit_b, a1, preferred_element_type=F32) + jnp.dot(hit_b, a2, preferred_element_type=F32)
             + jnp.dot(hit_b, a3, preferred_element_type=F32))
    gate_ref[...] = jnp.sum(jnp.where(c_iota == c_of_p, a_row, 0.0), axis=1, keepdims=True)


def _route(aff):
    b, t, ne = aff.shape
    cap = _moe_cap(t, ne)
    assert t % LANES == 0
    nr = t // LANES
    aff_t = jnp.swapaxes(aff, 1, 2).reshape(b, ne, nr, LANES)
    thr = pl.pallas_call(
        functools.partial(_route_threshold_kernel, cap=cap),
        grid=(b,),
        in_specs=[pl.BlockSpec((None, ne, nr, LANES), lambda bi: (bi, 0, 0, 0))],
        out_specs=pl.BlockSpec((None, ne, 8, LANES), lambda bi: (bi, 0, 0, 0)),
        out_shape=jax.ShapeDtypeStruct((b, ne, 8, LANES), jnp.int32),
        compiler_params=_cparams(("arbitrary",)),
        name="expert_choice_threshold",
    )(aff_t)
    out_blk = pl.BlockSpec((None, None, cap, 1), lambda bi, e: (bi, e, 0, 0))
    idx, gate = pl.pallas_call(
        functools.partial(_route_kernel, cap=cap),
        grid=(b, ne),
        in_specs=[pl.BlockSpec((None, None, nr, LANES), lambda bi, e: (bi, e, 0, 0)),
                  pl.BlockSpec((None, None, 8, LANES), lambda bi, e: (bi, e, 0, 0))],
        out_specs=[out_blk, out_blk],
        out_shape=[jax.ShapeDtypeStruct((b, ne, cap, 1), jnp.int32),
                   jax.ShapeDtypeStruct((b, ne, cap, 1), F32)],
        compiler_params=_cparams(("arbitrary", "arbitrary")),
        name="expert_choice_route",
    )(aff_t, thr)
    return idx.reshape(b, ne, cap), gate


def _moe_kernel(idx_ref, gate_ref, gt_ref, xh_in, wg_ref, wu_ref, wd_ref, xh_hbm,
                buf, obuf, hbuf, gsem, osem, *, tile, n_exp, n_tiles, n_steps, d, seq_len):
    del xh_in
    bi = pl.program_id(0)
    e = pl.program_id(1)
    ti = pl.program_id(2)
    per_batch = n_exp * n_tiles
    step = bi * per_batch + e * n_tiles + ti
    slot = step % 2

    def row_in(tok, k, b_, slot_):
        return pltpu.make_async_copy(xh_hbm.at[b_, pl.ds(tok, 1)], buf.at[slot_, pl.ds(k, 1)], gsem.at[slot_])

    def row_out(tok, k, b_, slot_):
        return pltpu.make_async_copy(obuf.at[slot_, pl.ds(k, 1)], xh_hbm.at[b_, pl.ds(tok, 1), pl.ds(0, d)],
                                     osem.at[slot_])

    def wait_rows_in(slot_):
        pltpu.make_async_copy(xh_hbm.at[0, pl.ds(0, tile)], buf.at[slot_], gsem.at[slot_]).wait()

    def wait_rows_out(slot_):
        pltpu.make_async_copy(obuf.at[slot_], xh_hbm.at[0, pl.ds(0, tile), pl.ds(0, d)], osem.at[slot_]).wait()

    def fetch_now():
        def body(k, carry):
            row_in(idx_ref[step * tile + k], k, bi, slot).start()
            return carry

        lax.fori_loop(0, tile, body, 0, unroll=8)

    starts_expert = (ti == 0) & (e > 0)
    ends_expert = ((ti == n_tiles - 1) & (e < n_exp - 1)) | (step == n_steps - 1)

    @pl.when(starts_expert)
    def _():
        wait_rows_out(slot)
        wait_rows_out(1 - slot)
        fetch_now()

    @pl.when(step == 0)
    def _():
        obuf[...] = jnp.zeros(obuf.shape, F32)
        fetch_now()

    wait_rows_in(slot)
    hbuf[...] = buf[slot, :, d:2 * d].astype(BF16)

    src_step = jnp.where(ends_expert, step, step + 1)
    src_b = src_step // per_batch
    for k in range(tile):
        row_in(idx_ref[src_step * tile + k], k, src_b, 1 - slot).start()
    prev_written = starts_expert | (step == 0)
    prev_step = jnp.maximum(step - 1, 0)
    prev_b = prev_step // per_batch
    for k in range(tile):
        tok = jnp.where(prev_written, seq_len + k, idx_ref[prev_step * tile + k])
        row_out(tok, k, prev_b, 1 - slot).start()

    h = hbuf[...]
    g = jnp.dot(h, wg_ref[...], preferred_element_type=F32)
    u = jnp.dot(h, wu_ref[...], preferred_element_type=F32)
    hid = (g * jax.nn.sigmoid(g)) * u
    y = jnp.dot(hid.astype(BF16), wd_ref[...], preferred_element_type=F32)
    new_rows = buf[slot, :, 0:d] + y * gate_ref[...] * gt_ref[...]

    @pl.when((step > 0) & jnp.logical_not(starts_expert))
    def _():
        wait_rows_out(slot)

    obuf[slot] = new_rows

    @pl.when(ends_expert)
    def _():
        wait_rows_in(1 - slot)

        def put(k, carry):
            row_out(idx_ref[step * tile + k], k, bi, slot).start()
            return carry

        lax.fori_loop(0, tile, put, 0, unroll=8)

    @pl.when(step == n_steps - 1)
    def _():
        wait_rows_out(slot)
        wait_rows_out(1 - slot)


def _moe(xh, seq_len, idx, gate, gt_f, w_gate, w_up, w_down, layer):
    b, t_pad, d2 = xh.shape
    d = d2 // 2
    _, ne, _, ff = w_gate.shape
    cap = idx.shape[2]
    tile = _moe_tile(cap)
    assert cap == _moe_cap(seq_len, ne) and t_pad == _xh_rows(seq_len, ne)
    nt = cap // tile
    n_steps = b * ne * nt
    grid_spec = pltpu.PrefetchScalarGridSpec(
        num_scalar_prefetch=1,
        grid=(b, ne, nt),
        in_specs=[pl.BlockSpec((None, None, tile, 1), lambda bi, e, ti, idx_r: (bi, e, ti, 0)),
                  pl.BlockSpec((None, 1, d), lambda bi, e, ti, idx_r: (bi, 0, 0)),
                  pl.BlockSpec(memory_space=pl.ANY),
                  pl.BlockSpec((None, None, d, ff), lambda bi, e, ti, idx_r: (layer, e, 0, 0)),
                  pl.BlockSpec((None, None, d, ff), lambda bi, e, ti, idx_r: (layer, e, 0, 0)),
                  pl.BlockSpec((None, None, ff, d), lambda bi, e, ti, idx_r: (layer, e, 0, 0))],
        out_specs=pl.BlockSpec(memory_space=pl.ANY),
        scratch_shapes=[pltpu.VMEM((2, tile, d2), F32), pltpu.VMEM((2, tile, d), F32), pltpu.VMEM((tile, d), BF16),
                        pltpu.SemaphoreType.DMA((2,)), pltpu.SemaphoreType.DMA((2,))],
    )
    return pl.pallas_call(
        functools.partial(_moe_kernel, tile=tile, n_exp=ne, n_tiles=nt, n_steps=n_steps, d=d, seq_len=seq_len),
        grid_spec=grid_spec,
        out_shape=jax.ShapeDtypeStruct((b, t_pad, d2), F32),
        input_output_aliases={3: 0},
        compiler_params=_cparams(("arbitrary", "arbitrary", "arbitrary")),
        name="expert_ffn",
    )(idx.reshape(-1), gate, gt_f, xh, w_gate, w_up, w_down)


def _final_norm_kernel(x_ref, g_ref, o_ref):
    x = x_ref[...]
    o_ref[...] = x * lax.rsqrt(jnp.mean(x * x, axis=-1, keepdims=True) + NORM_EPS) * g_ref[...]


def _final_norm(x, t, g):
    b = x.shape[0]
    d = g.shape[0]
    tm = _tile(t, 512)
    xt = pl.BlockSpec((None, tm, d), lambda bi, i: (bi, i, 0))
    return pl.pallas_call(
        _final_norm_kernel,
        grid=(b, t // tm),
        in_specs=[xt, pl.BlockSpec((1, d), lambda bi, i: (0, 0))],
        out_specs=xt,
        out_shape=jax.ShapeDtypeStruct((b, t, d), F32),
        compiler_params=_cparams(("arbitrary", "arbitrary")),
        name="final_norm",
    )(x, g.reshape(1, d))


def _moe_layer(xh, aff, gt_f, w_gate, w_up, w_down, layer):
    idx, gate = _route(aff)
    return _moe(xh, aff.shape[1], idx, gate, gt_f, w_gate, w_up, w_down, layer)


def kernel(x, c, ctx, c_ctx, ada_w, ada_b, norm_mix, norm_ffn, w_in, shift_mu, decay_w0, decay_w2, iclr_a0,
           iclr_a2, gate_g2, k_k, k_a, r_k, ln_w, ln_b, sink, w_out, pool_w, pool_scale, router_w, exp_w_gate,
           exp_w_up, exp_w_down, norm_final):
    b, t, d = x.shape
    depth = ada_w.shape[0]
    assert depth == 2, "the context stream is only advanced for deeper stacks; not implemented"
    assert b + 1 <= 8
    cc = jnp.concatenate([c, c_ctx[None, :], jnp.zeros((8 - b - 1, d), F32)], axis=0)
    mod = _ada(cc, ada_w, ada_b)

    def mods(l, rows):
        return [mod[l, rows, None, i * d:(i + 1) * d] for i in range(6)]

    cos_t, sin_t = _rope_tables(t)
    wg_b, wu_b, wd_b = exp_w_gate.astype(BF16), exp_w_up.astype(BF16), exp_w_down.astype(BF16)
    for l in range(depth):
        sh_a, sc_a, gt_a, sh_f, sc_f, gt_f = mods(l, slice(0, b))
        if l % 2 == 0:
            e = l // 2
            csh_a, csc_a = [jnp.broadcast_to(m, (b, 1, d)) for m in mods(l, slice(b, b + 1))[:2]]
            w_p = _relayout_in_cols(w_in[e]).astype(BF16)
            mu_p = _relayout_in_cols(jnp.concatenate(
                [shift_mu[e], jnp.zeros((w_in.shape[2] - shift_mu.shape[1],), F32)]))
            mu_rkv = mu_p[None, :3 * RW]
            mu_lo = mu_p[None, COL_LORA:]
            px = _inproj(x, norm_mix[l], sc_a, sh_a, w_p, cos_t, sin_t)
            pc = _inproj(ctx, norm_mix[l], csc_a, csh_a, w_p)
            scan_args = (mu_rkv, mu_lo, decay_w0[e][:, None, :], decay_w2[e], iclr_a0[e][:, None, :], iclr_a2[e],
                         k_k[e][None], k_a[e][None], r_k[e][None])
            s0 = jnp.zeros((b, 2, HEADS // QUAD, QW, QW), F32)
            s_ctx = _rwkv_scan(pc, s0, *scan_args)[4]
            yf, yb, bf, bb, _ = _rwkv_scan(px, s_ctx, *scan_args)
            g2_pad = jnp.zeros((LORA_W, RW), F32).at[LORA_GATE:LORA_GATE + GATE_LORA].set(gate_g2[e]).astype(BF16)
            rwkv_x = _rwkv_finish(yf, yb, bf, bb, px, mu_lo, g2_pad, ln_w[e][None], ln_b[e][None])
            att_x = _attention(px, pc, sink[e])
            xh, aff = _outproj(rwkv_x, att_x, w_out[e].astype(BF16), x, gt_a, norm_ffn[l], sc_f, sh_f,
                               router_w[l].astype(BF16))
        else:
            o = l // 2
            xh, aff = _pool_layer(x, t, norm_mix[l], sc_a, sh_a, pool_w[o], pool_scale[o], gt_a, norm_ffn[l],
                                  sc_f, sh_f, router_w[l].astype(BF16))
        x = _moe_layer(xh, aff, gt_f, wg_b, wu_b, wd_b, l)
    return _final_norm(x, t, norm_final)
```

```python
import functools

import jax
import jax.numpy as jnp
from jax import lax
from jax.experimental import pallas as pl
from jax.experimental.pallas import tpu as pltpu

F32 = jnp.float32
BF16 = jnp.bfloat16

GRID_W = 64
HEADS = 16
HEAD = 64
RW = HEADS * HEAD
DECAY_LORA = 64
AAA_LORA = 64
GATE_LORA = 160
GN_EPS = 64e-5
KV_HEADS = 4
GROUP = HEADS // KV_HEADS
KVW = KV_HEADS * HEAD
ATT_SCALE = HEAD ** -0.5
WINDOW = 128
BLOCK = 128
ROPE_THETA = 10000.0
POOL_WINDOWS = (2, 4, 8, 16)
N_EXPERTS = 16
CAPACITY_FACTOR = 2
NORM_EPS = 1e-6

COL_Q = 3 * RW
COL_AK = 4 * RW
COL_AV = 4 * RW + KVW
COL_LORA = 4 * RW + 2 * KVW
LORA_W = 512
LORA_GATE = 2 * DECAY_LORA + 2 * AAA_LORA
CHUNK = 64
QUAD = 4
QW = QUAD * HEAD
INTERLEAVE = 8
assert CHUNK == HEAD
HALO = 8
LANES = 128
VMEM_LIMIT = 56 * 1024 * 1024
DMA_PRIORITIES = 2


def _cparams(sem):
    return pltpu.CompilerParams(dimension_semantics=sem, vmem_limit_bytes=VMEM_LIMIT)


def _tile(n, pref):
    t = min(n, pref)
    assert n % t == 0, (n, t)
    return t


def _moe_cap(t, ne):
    return max(1, CAPACITY_FACTOR * t // ne)


def _moe_tile(cap):
    return _tile(cap, 256)


def _row_tile(t):
    return _tile(t, 256)


def _xh_rows(t, ne):
    assert _moe_tile(_moe_cap(t, ne)) <= _row_tile(t)
    return t + _row_tile(t)


def _bdot(a, b):
    return jnp.dot(a.astype(BF16), b.astype(BF16), preferred_element_type=F32)


def _bdot_nt(a, b):
    return lax.dot_general(a.astype(BF16), b.astype(BF16), (((1,), (1,)), ((), ())),
                           preferred_element_type=F32)


def _bf16_terms(x, n):
    terms = []
    for _ in range(n):
        t = x.astype(BF16)
        terms.append(t)
        x = x - t.astype(F32)
    return terms


def _rms_mod(x, g, sc, sh):
    y = x * lax.rsqrt(jnp.mean(x * x, axis=-1, keepdims=True) + NORM_EPS)
    return (y * g) * (1.0 + sc) + sh


def _head_sum(x, bd):
    cols = []
    for g in range(x.shape[1] // LANES):
        cols.append(sum(jnp.dot(t, bd, preferred_element_type=F32)
                        for t in _bf16_terms(x[:, g * LANES:(g + 1) * LANES], 2)))
    return jnp.concatenate(cols, axis=1)


def _block_diag_ones():
    r = lax.broadcasted_iota(jnp.int32, (LANES, LANES), 0) // HEAD
    c = lax.broadcasted_iota(jnp.int32, (LANES, LANES), 1) // HEAD
    return jnp.where(r == c, 1.0, 0.0).astype(BF16)


def _ada_kernel(cc_ref, w_ref, b_ref, o_ref):
    cc = cc_ref[...]
    s = cc * jax.nn.sigmoid(cc)
    o_ref[...] = _bdot(s, w_ref[...]) + b_ref[...]


def _ada(cc, ada_w, ada_b):
    n_l, d, n = ada_w.shape
    tn = _tile(n, 1536)
    return pl.pallas_call(
        _ada_kernel,
        grid=(n_l, n // tn),
        in_specs=[pl.BlockSpec((8, d), lambda l, j: (0, 0)),
                  pl.BlockSpec((None, d, tn), lambda l, j: (l, 0, j)),
                  pl.BlockSpec((None, 1, tn), lambda l, j: (l, 0, j))],
        out_specs=pl.BlockSpec((None, 8, tn), lambda l, j: (l, 0, j)),
        out_shape=jax.ShapeDtypeStruct((n_l, 8, n), F32),
        compiler_params=_cparams(("arbitrary", "arbitrary")),
        name="adaln",
    )(cc, ada_w, ada_b.reshape(n_l, 1, n))


def _inproj_kernel(*refs, rope_lo, rope_hi, tn):
    if rope_hi > rope_lo:
        x_ref, g_ref, sc_ref, sh_ref, w_ref, cos_ref, sin_ref, o_ref, h_scr = refs
    else:
        x_ref, g_ref, sc_ref, sh_ref, w_ref, o_ref, h_scr = refs
    j = pl.program_id(2)

    @pl.when(j == 0)
    def _():
        h_scr[...] = _rms_mod(x_ref[...], g_ref[...], sc_ref[...], sh_ref[...]).astype(BF16)

    acc = jnp.dot(h_scr[...], w_ref[...], preferred_element_type=F32)
    if rope_hi > rope_lo:
        def store_with_rope(n_groups):
            cos = cos_ref[...]
            sin = sin_ref[...]
            lane = lax.broadcasted_iota(jnp.int32, cos.shape, 1)
            first = (lane % 32) < 16
            outs = []
            for g in range(n_groups):
                a = acc[:, g * LANES:(g + 1) * LANES]
                partner = jnp.where(first, pltpu.roll(a, LANES - 16, 1), pltpu.roll(a, 16, 1))
                outs.append(a * cos + partner * sin)
            if n_groups < tn // LANES:
                outs.append(acc[:, n_groups * LANES:])
            o_ref[...] = jnp.concatenate(outs, axis=1)

        is_q = (j >= rope_lo) & (j < rope_hi)
        is_k = j == rope_hi

        @pl.when(is_q)
        def _():
            store_with_rope(tn // LANES)

        @pl.when(is_k)
        def _():
            store_with_rope(KVW // LANES)

        @pl.when(jnp.logical_not(is_q | is_k))
        def _():
            o_ref[...] = acc
    else:
        o_ref[...] = acc


def _inproj(x, g, sc, sh, w, cos_t=None, sin_t=None):
    b, t, d = x.shape
    n = w.shape[1]
    tm = _tile(t, 1024)
    tn = 1024
    assert COL_Q % tn == 0 and COL_AK % tn == 0 and KVW <= tn
    rope = cos_t is not None
    rope_lo, rope_hi = (COL_Q // tn, COL_AK // tn) if rope else (0, 0)
    in_specs = [pl.BlockSpec((None, tm, d), lambda bi, i, j: (bi, i, 0)),
                pl.BlockSpec((1, d), lambda bi, i, j: (0, 0)),
                pl.BlockSpec((None, 1, d), lambda bi, i, j: (bi, 0, 0)),
                pl.BlockSpec((None, 1, d), lambda bi, i, j: (bi, 0, 0)),
                pl.BlockSpec((d, tn), lambda bi, i, j: (0, j))]
    args = [x, g.reshape(1, d), sc, sh, w]
    if rope:
        in_specs += [pl.BlockSpec((tm, LANES), lambda bi, i, j: (i, 0)),
                     pl.BlockSpec((tm, LANES), lambda bi, i, j: (i, 0))]
        args += [cos_t, sin_t]
    return pl.pallas_call(
        functools.partial(_inproj_kernel, rope_lo=rope_lo, rope_hi=rope_hi, tn=tn),
        grid=(b, t // tm, n // tn),
        in_specs=in_specs,
        out_specs=pl.BlockSpec((None, tm, tn), lambda bi, i, j: (bi, i, j)),
        out_shape=jax.ShapeDtypeStruct((b, t, n), F32),
        scratch_shapes=[pltpu.VMEM((tm, d), BF16)],
        compiler_params=_cparams(("arbitrary", "arbitrary", "arbitrary")),
        name="inproj_rope" if rope else "inproj_ctx",
    )(*args)


def _shift_lerp(p, prev_blk, next_blk, mu, has_prev, has_next):
    rows = p.shape[0]
    ridx = lax.broadcasted_iota(jnp.int32, p.shape, 0)
    prev_row = jnp.where(has_prev, prev_blk[HALO - 1:HALO, :], 0.0)
    next_row = jnp.where(has_next, next_blk[0:1, :], 0.0)
    prev = jnp.where(ridx == 0, prev_row, pltpu.roll(p, 1, 0))
    nxt = jnp.where(ridx == rows - 1, next_row, pltpu.roll(p, rows - 1, 0))
    return p + mu * (0.5 * (prev + nxt) - p)


def _rwkv_kernel(*refs, n_chunks):
    (rkv_f, lo_f, rkv_fp, lo_fp, rkv_fn, lo_fn,
     rkv_b, lo_b, rkv_bp, lo_bp, rkv_bn, lo_bn,
     mu_rkv_ref, mu_lo_ref, w0_ref, w2_ref, a0_ref, a2_ref, kk_ref, ka_ref, rk_ref, s0_ref,
     yf_ref, yb_ref, bf_ref, bb_ref, sfin_ref, state) = refs
    c = pl.program_id(1)
    cn = CHUNK

    @pl.when(c == 0)
    def _():
        state[...] = s0_ref[...]

    bd = _block_diag_ones()
    ri = lax.broadcasted_iota(jnp.int32, (cn, cn), 0)
    ci = lax.broadcasted_iota(jnp.int32, (cn, cn), 1)
    qr = lax.broadcasted_iota(jnp.int32, (QW, QW), 0)
    qc = lax.broadcasted_iota(jnp.int32, (QW, QW), 1)
    same_head = (qr // HEAD) == (qc // HEAD)
    eye = (qr == qc).astype(F32)
    lane_head = lax.broadcasted_iota(jnp.int32, (cn, QW), 1) // HEAD
    first_half = lax.broadcasted_iota(jnp.int32, (2 * QW, 2 * cn), 1) < cn
    mu_rkv = mu_rkv_ref[...]
    mu_lo = mu_lo_ref[...]
    k_k = kk_ref[...]
    k_a = ka_ref[...]
    r_k = rk_ref[...]

    dirs = ((0, c, rkv_f, lo_f, rkv_fp, lo_fp, rkv_fn, lo_fn, yf_ref, bf_ref),
            (1, n_chunks - 1 - c, rkv_b, lo_b, rkv_bp, lo_bp, rkv_bn, lo_bn, yb_ref, bb_ref))
    prep = []
    for d, cd, rkv_ref, lo_ref, rkv_p, lo_p, rkv_n, lo_n, y_ref, bon_ref in dirs:
        has_prev = cd > 0
        has_next = cd < n_chunks - 1
        m = _shift_lerp(rkv_ref[...], rkv_p[...], rkv_n[...], mu_rkv, has_prev, has_next)
        ml = _shift_lerp(lo_ref[...], lo_p[...], lo_n[...], mu_lo, has_prev, has_next)
        r = m[:, 0:RW]
        k = m[:, RW:2 * RW]
        v = m[:, 2 * RW:3 * RW]
        wd = ml[:, d * DECAY_LORA:(d + 1) * DECAY_LORA]
        ad = ml[:, 2 * DECAY_LORA + d * AAA_LORA:2 * DECAY_LORA + (d + 1) * AAA_LORA]
        w_pre = w0_ref[d] + _bdot(jnp.tanh(wd), w2_ref[d])
        z = -w_pre
        softplus = jnp.maximum(z, 0.0) + jnp.log(1.0 + jnp.exp(-jnp.abs(z)))
        logw = -jnp.exp(-softplus - 0.5)
        a = jax.nn.sigmoid(a0_ref[d] + _bdot(ad, a2_ref[d]))
        kk = k * k_k
        kk = kk * lax.rsqrt(jnp.maximum(_head_sum(kk * kk, bd), 1e-24))
        k_mod = k * (1.0 + (a - 1.0) * k_a)
        bon_ref[...] = _head_sum(r * k_mod * r_k, bd) * v

        if d == 0:
            tri = jnp.where(ci <= ri, 1.0, 0.0).astype(BF16)
            strict = (qc % cn) < (qr % cn)
            incl = (qc % cn) <= (qr % cn)
            last = cn - 1
        else:
            tri = jnp.where(ci >= ri, 1.0, 0.0).astype(BF16)
            strict = (qc % cn) > (qr % cn)
            incl = (qc % cn) >= (qr % cn)
            last = 0
        cum = sum(jnp.dot(tri, t, preferred_element_type=F32)
                  for t in _bf16_terms(logw, 3))
        e_pos = jnp.exp(cum)
        e_neg = jnp.exp(-cum)
        p_last = e_pos[last:last + 1, :]
        bh = (kk * a) * e_neg
        kh = k_mod * e_neg
        prep.append(dict(d=d, at=(-kk) * jnp.exp(cum - logw), rt=r * e_pos, bh=bh, kh=kh, v=v,
                         bbar=bh * p_last, kbar=kh * p_last, p_last=p_last,
                         m_strict=same_head & strict, m_incl=same_head & incl, y_ref=y_ref))

    def rep(x):
        return jnp.concatenate([x] * QUAD, axis=0)

    def to_bd(x):
        return jnp.where(same_head, rep(x), 0.0)

    def collapse(x):
        out = jnp.where(lane_head == 0, x[0:cn], 0.0)
        for h in range(1, QUAD):
            out = out + jnp.where(lane_head == h, x[h * cn:(h + 1) * cn], 0.0)
        return out

    all_items = []
    for pr in prep:
        for q in range(HEADS // QUAD):
            sl = slice(q * QW, (q + 1) * QW)
            it = {name: pr[name][:, sl] for name in ("at", "rt", "bh", "kh", "v", "bbar", "kbar", "p_last")}
            it.update(d=pr["d"], q=q, m_strict=pr["m_strict"], m_incl=pr["m_incl"])
            all_items.append(it)

    def run_stages(items):
        for it in items:
            g = _bdot_nt(jnp.concatenate([to_bd(it["at"]), to_bd(it["rt"])], axis=0),
                         jnp.concatenate([it["bh"], it["kh"]], axis=0))
            g_sw = pltpu.roll(g, cn, 1)
            gb = jnp.where(first_half, g, g_sw)
            gk = jnp.where(first_half, g_sw, g)
            gb = jnp.concatenate([gb] * (QW // (2 * cn)), axis=1)
            gk = jnp.concatenate([gk] * (QW // (2 * cn)), axis=1)
            it["a_ak"] = jnp.where(it["m_strict"], gk[:QW], 0.0)
            it["a_rb"] = jnp.where(it["m_incl"], gb[QW:], 0.0)
            it["a_rk"] = jnp.where(it["m_incl"], gk[QW:], 0.0)
            it["p"] = jnp.where(it["m_strict"], gb[:QW], 0.0)
            it["t"] = eye + it["p"]
        for it in items:
            pb = it["p"].astype(BF16)
            it["p"] = jnp.dot(pb, pb, preferred_element_type=F32)
        for _ in range(4):
            for it in items:
                pb = it["p"].astype(BF16)
                tp = jnp.dot(jnp.concatenate([it["t"].astype(BF16), pb], axis=0), pb,
                             preferred_element_type=F32)
                it["t"] = it["t"] + tp[:QW]
                it["p"] = tp[QW:]
        for it in items:
            it["t"] = it["t"] + _bdot(it["t"], it["p"])
        for it in items:
            av = _bdot(jnp.concatenate([it["a_ak"], it["a_rk"]], axis=0), rep(it["v"].astype(BF16)))
            it["aakv"] = collapse(av[:QW])
            it["y0"] = collapse(av[QW:])
        for it in items:
            it["xt"] = jnp.concatenate([it["bbar"], it["kbar"]], axis=0).T
            it["pcol"] = jnp.broadcast_to(it["p_last"], (LANES, QW)).T[:, 0:1]
            it["t"] = it["t"].astype(BF16)
        for it in items:
            it["s"] = state[it["d"], it["q"]]
        for it in items:
            zs = _bdot(jnp.concatenate([it["at"], it["rt"]], axis=0), it["s"])
            it["z"] = zs[:cn] + it["aakv"]
            it["ys"] = zs[cn:]
        for it in items:
            it["u"] = collapse(_bdot(it["t"], rep(it["z"].astype(BF16))))
        for it in items:
            it["y"] = it["ys"] + collapse(_bdot(it["a_rb"], rep(it["u"].astype(BF16)))) + it["y0"]
        for it in items:
            upd = _bdot(it["xt"], jnp.concatenate([it["u"], it["v"]], axis=0))
            state[it["d"], it["q"]] = it["s"] * it["pcol"] + jnp.where(same_head, upd, 0.0)

    for first in range(0, len(all_items), INTERLEAVE):
        run_stages(all_items[first:first + INTERLEAVE])
    for pr in prep:
        pr["y_ref"][...] = jnp.concatenate([it["y"] for it in all_items if it["d"] == pr["d"]], axis=1)

    @pl.when(c == n_chunks - 1)
    def _():
        sfin_ref[...] = state[...]


def _rwkv_scan(px, s0, mu_rkv, mu_lo, w0, w2, a0, a2, k_k, k_a, r_k):
    b, t, _ = px.shape
    cn = CHUNK
    nc = t // cn
    assert t % cn == 0 and cn % HALO == 0
    hb = cn // HALO
    n_halo = t // HALO
    lora_blk = COL_LORA // LORA_W

    def fwd(ci):
        return ci

    def bwd(ci):
        return nc - 1 - ci

    def specs(cmap):
        return [
            pl.BlockSpec((None, cn, 3 * RW), lambda bi, ci: (bi, cmap(ci), 0)),
            pl.BlockSpec((None, cn, LORA_W), lambda bi, ci: (bi, cmap(ci), lora_blk)),
            pl.BlockSpec((None, HALO, 3 * RW), lambda bi, ci: (bi, jnp.maximum(cmap(ci) * hb - 1, 0), 0)),
            pl.BlockSpec((None, HALO, LORA_W),
                         lambda bi, ci: (bi, jnp.maximum(cmap(ci) * hb - 1, 0), lora_blk)),
            pl.BlockSpec((None, HALO, 3 * RW),
                         lambda bi, ci: (bi, jnp.minimum((cmap(ci) + 1) * hb, n_halo - 1), 0)),
            pl.BlockSpec((None, HALO, LORA_W),
                         lambda bi, ci: (bi, jnp.minimum((cmap(ci) + 1) * hb, n_halo - 1), lora_blk)),
        ]

    def const(shape):
        nd = len(shape)
        return pl.BlockSpec(shape, lambda bi, ci: (0,) * nd)

    in_specs = specs(fwd) + specs(bwd) + [
        const((1, 3 * RW)), const((1, LORA_W)),
        const((2, 1, RW)), const((2, DECAY_LORA, RW)), const((2, 1, RW)), const((2, AAA_LORA, RW)),
        const((1, RW)), const((1, RW)), const((1, RW)),
        pl.BlockSpec((None, 2, HEADS // QUAD, QW, QW), lambda bi, ci: (bi, 0, 0, 0, 0)),
    ]
    yshape = jax.ShapeDtypeStruct((b, t, RW), F32)
    out_specs = [
        pl.BlockSpec((None, cn, RW), lambda bi, ci: (bi, ci, 0)),
        pl.BlockSpec((None, cn, RW), lambda bi, ci: (bi, nc - 1 - ci, 0)),
        pl.BlockSpec((None, cn, RW), lambda bi, ci: (bi, ci, 0)),
        pl.BlockSpec((None, cn, RW), lambda bi, ci: (bi, nc - 1 - ci, 0)),
        pl.BlockSpec((None, 2, HEADS // QUAD, QW, QW), lambda bi, ci: (bi, 0, 0, 0, 0)),
    ]
    px_args = [px] * 12
    return pl.pallas_call(
        functools.partial(_rwkv_kernel, n_chunks=nc),
        grid=(b, nc),
        in_specs=in_specs,
        out_specs=out_specs,
        out_shape=[yshape, yshape, yshape, yshape,
                   jax.ShapeDtypeStruct((b, 2, HEADS // QUAD, QW, QW), F32)],
        scratch_shapes=[pltpu.VMEM((2, HEADS // QUAD, QW, QW), F32)],
        compiler_params=_cparams(("arbitrary", "arbitrary")),
        name="rwkv_scan",
    )(*px_args, mu_rkv, mu_lo, w0, w2.astype(BF16), a0, a2.astype(BF16), k_k, k_a, r_k, s0)


def _relayout_in_cols(a):
    n_lora = 2 * DECAY_LORA + 2 * AAA_LORA + GATE_LORA
    rkv = a[..., :3 * RW]
    lora = a[..., 3 * RW:3 * RW + n_lora]
    att = a[..., 3 * RW + n_lora:]
    pad = jnp.zeros(a.shape[:-1] + (LORA_W - n_lora,), a.dtype)
    return jnp.concatenate([rkv, att, lora, pad], axis=-1)


def _rope_tables(t):
    rows = t // GRID_W
    row = jnp.repeat(jnp.arange(rows, dtype=F32), GRID_W)
    col = jnp.tile(jnp.arange(GRID_W, dtype=F32), rows)
    n_freq = HEAD // 4
    inv = ROPE_THETA ** (-jnp.arange(n_freq, dtype=F32) / n_freq)
    ar = row[:, None] * inv
    ac = col[:, None] * inv
    cos = jnp.concatenate([jnp.cos(ar), jnp.cos(ar), jnp.cos(ac), jnp.cos(ac)], axis=-1)
    sin = jnp.concatenate([-jnp.sin(ar), jnp.sin(ar), -jnp.sin(ac), jnp.sin(ac)], axis=-1)
    reps = LANES // HEAD
    return jnp.tile(cos, (1, reps)), jnp.tile(sin, (1, reps))


def _finish_kernel(yf_ref, yb_ref, bf_ref, bb_ref, lo_ref, lo_p, lo_n, mu_lo_ref, g2_ref, lnw_ref, lnb_ref,
                   o_ref, *, n_tiles):
    i = pl.program_id(1)
    bd = _block_diag_ones()
    y = yf_ref[...] + yb_ref[...]
    mean = _head_sum(y, bd) * (1.0 / HEAD)
    yc = y - mean
    var = _head_sum(yc * yc, bd) * (1.0 / HEAD)
    yn = yc * lax.rsqrt(var + GN_EPS) * lnw_ref[...] + lnb_ref[...]
    ml = _shift_lerp(lo_ref[...], lo_p[...], lo_n[...], mu_lo_ref[...], i > 0, i < n_tiles - 1)
    gate = _bdot(jax.nn.sigmoid(ml), g2_ref[...])
    o_ref[...] = ((yn + bf_ref[...] + bb_ref[...]) * gate).astype(o_ref.dtype)


def _rwkv_finish(yf, yb, bf, bb, px, mu_lo, g2_pad, ln_w, ln_b):
    b, t, _ = yf.shape
    tm = _tile(t, 512)
    nt = t // tm
    hb = tm // HALO
    n_halo = t // HALO
    lora_blk = COL_LORA // LORA_W
    tile = pl.BlockSpec((None, tm, RW), lambda bi, i: (bi, i, 0))
    vec = pl.BlockSpec((1, RW), lambda bi, i: (0, 0))
    return pl.pallas_call(
        functools.partial(_finish_kernel, n_tiles=nt),
        grid=(b, nt),
        in_specs=[tile, tile, tile, tile,
                  pl.BlockSpec((None, tm, LORA_W), lambda bi, i: (bi, i, lora_blk)),
                  pl.BlockSpec((None, HALO, LORA_W), lambda bi, i: (bi, jnp.maximum(i * hb - 1, 0), lora_blk)),
                  pl.BlockSpec((None, HALO, LORA_W),
                               lambda bi, i: (bi, jnp.minimum((i + 1) * hb, n_halo - 1), lora_blk)),
                  pl.BlockSpec((1, LORA_W), lambda bi, i: (0, 0)),
                  pl.BlockSpec((LORA_W, RW), lambda bi, i: (0, 0)),
                  vec, vec],
        out_specs=tile,
        out_shape=jax.ShapeDtypeStruct((b, t, RW), BF16),
        compiler_params=_cparams(("arbitrary", "arbitrary")),
        name="rwkv_finish",
    )(yf, yb, bf, bb, px, px, px, mu_lo, g2_pad, ln_w, ln_b)


def _attn_kernel(q_ref, kp_ref, kc_ref, kn_ref, vp_ref, vc_ref, vn_ref, ck_ref, cv_ref, sink_ref, o_ref,
                 *, n_blocks, n_ctx):
    n = pl.program_id(1)
    q = q_ref[...] * ATT_SCALE
    k_win = jnp.concatenate([kp_ref[...], kc_ref[...], kn_ref[...]], axis=0)
    v_win = jnp.concatenate([vp_ref[...], vc_ref[...], vn_ref[...]], axis=0)
    k_all = jnp.concatenate([ck_ref[...], k_win], axis=0).astype(BF16)
    v_all = jnp.concatenate([cv_ref[...], v_win], axis=0).astype(BF16)
    rows = GROUP * BLOCK
    span = n_ctx + 3 * BLOCK
    qi = lax.broadcasted_iota(jnp.int32, (rows, span), 0) % BLOCK + BLOCK
    kj = lax.broadcasted_iota(jnp.int32, (rows, span), 1) - n_ctx
    k_lo = jnp.where(n == 0, BLOCK, 0)
    k_hi = jnp.where(n == n_blocks - 1, 2 * BLOCK, 3 * BLOCK)
    in_win = (jnp.abs(qi - kj) <= WINDOW) & (kj >= k_lo) & (kj < k_hi)
    ok = (kj < 0) | in_win
    sink = sink_ref[...]
    groups = range(KV_HEADS)
    s_all, sk_all, m_all, p_all, outs = [], [], [], [], []
    for g in groups:
        qg = jnp.concatenate([q[:, (g * GROUP + i) * HEAD:(g * GROUP + i + 1) * HEAD] for i in range(GROUP)],
                             axis=0)
        s_all.append(jnp.where(ok, _bdot_nt(qg, k_all[:, g * HEAD:(g + 1) * HEAD]), -jnp.inf))
        sk_all.append(jnp.concatenate([jnp.broadcast_to(sink[:, g * GROUP + i:g * GROUP + i + 1], (BLOCK, 1))
                                       for i in range(GROUP)], axis=0))
    for g in groups:
        m_all.append(jnp.maximum(jnp.max(s_all[g], axis=-1, keepdims=True), sk_all[g]))
    for g in groups:
        p_all.append(jnp.exp((s_all[g] - m_all[g]).astype(BF16)))
    for g in groups:
        vg = jnp.concatenate([v_all[:, g * HEAD:(g + 1) * HEAD], jnp.ones((span, HEAD), BF16)], axis=1)
        og = jnp.dot(p_all[g], vg, preferred_element_type=F32)
        denom = og[:, HEAD:HEAD + 1] + jnp.exp(sk_all[g] - m_all[g])
        og = og[:, :HEAD] / denom
        outs += [og[i * BLOCK:(i + 1) * BLOCK] for i in range(GROUP)]
    o_ref[...] = jnp.concatenate(outs, axis=1).astype(o_ref.dtype)


def _attention(px, pc, sink):
    b, t, _ = px.shape
    n_ctx = pc.shape[1]
    nb = t // BLOCK
    kb = COL_AK // KVW
    vb = COL_AV // KVW

    def blk(col, off):
        return pl.BlockSpec((None, BLOCK, KVW), lambda bi, n: (bi, jnp.clip(n + off, 0, nb - 1), col))

    return pl.pallas_call(
        functools.partial(_attn_kernel, n_blocks=nb, n_ctx=n_ctx),
        grid=(b, nb),
        in_specs=[pl.BlockSpec((None, BLOCK, RW), lambda bi, n: (bi, n, COL_Q // RW)),
                  blk(kb, -1), blk(kb, 0), blk(kb, 1), blk(vb, -1), blk(vb, 0), blk(vb, 1),
                  pl.BlockSpec((None, n_ctx, KVW), lambda bi, n: (bi, 0, kb)),
                  pl.BlockSpec((None, n_ctx, KVW), lambda bi, n: (bi, 0, vb)),
                  pl.BlockSpec((1, HEADS), lambda bi, n: (0, 0))],
        out_specs=pl.BlockSpec((None, BLOCK, RW), lambda bi, n: (bi, n, 0)),
        out_shape=jax.ShapeDtypeStruct((b, t, RW), BF16),
        compiler_params=_cparams(("arbitrary", "arbitrary")),
        name="window_attention",
    )(px, px, px, px, px, px, px, pc, pc, sink.reshape(1, HEADS))


def _ffn_pre(x_new, gf_ref, scf_ref, shf_ref, rw_ref, xh_out, aff_out):
    d = x_new.shape[1]
    xh_out[:, 0:d] = x_new
    hf = _rms_mod(x_new, gf_ref[...], scf_ref[...], shf_ref[...])
    xh_out[:, d:2 * d] = hf
    logits = _bdot(hf, rw_ref[...])
    e = jnp.exp(logits - jnp.max(logits, axis=-1, keepdims=True))
    aff_out[...] = e / jnp.sum(e, axis=-1, keepdims=True)


def _outproj_kernel(ra_ref, at_ref, w_ref, x_ref, gt_ref, gf_ref, scf_ref, shf_ref, rw_ref,
                    xh_out, aff_out, *, n_tiles):
    i = pl.program_id(1)

    @pl.when(i < n_tiles)
    def _():
        y = (jnp.dot(ra_ref[...], w_ref[0:RW, :], preferred_element_type=F32)
             + jnp.dot(at_ref[...], w_ref[RW:2 * RW, :], preferred_element_type=F32))
        _ffn_pre(x_ref[...] + gt_ref[...] * y, gf_ref, scf_ref, shf_ref, rw_ref, xh_out, aff_out)

    @pl.when(i == n_tiles)
    def _():
        xh_out[...] = jnp.zeros(xh_out.shape, F32)


def _mod_specs(d):
    return pl.BlockSpec((None, 1, d), lambda bi, i: (bi, 0, 0))


def _outproj(rwkv_x, att_x, w_out, x, gt_a, g_ffn, sc_f, sh_f, router_w):
    b, t, d = x.shape
    ne = router_w.shape[1]
    tm = _row_tile(t)
    nt = t // tm
    assert _xh_rows(t, ne) == t + tm

    def row(bi, i):
        return (bi, jnp.minimum(i, nt - 1), 0)

    xt = pl.BlockSpec((None, tm, d), row)
    mt = pl.BlockSpec((None, tm, RW), row)
    return pl.pallas_call(
        functools.partial(_outproj_kernel, n_tiles=nt),
        grid=(b, nt + 1),
        in_specs=[mt, mt, pl.BlockSpec((2 * RW, d), lambda bi, i: (0, 0)), xt, _mod_specs(d),
                  pl.BlockSpec((1, d), lambda bi, i: (0, 0)), _mod_specs(d), _mod_specs(d),
                  pl.BlockSpec((d, ne), lambda bi, i: (0, 0))],
        out_specs=[pl.BlockSpec((None, tm, 2 * d), lambda bi, i: (bi, i, 0)),
                   pl.BlockSpec((None, tm, ne), row)],
        out_shape=[jax.ShapeDtypeStruct((b, _xh_rows(t, ne), 2 * d), F32), jax.ShapeDtypeStruct((b, t, ne), F32)],
        compiler_params=_cparams(("arbitrary", "arbitrary")),
        name="outproj_ffnpre",
    )(rwkv_x, att_x, w_out, x, gt_a, g_ffn.reshape(1, d), sc_f, sh_f, router_w)


def _pool_kernel(*refs, tm, seq_len):
    i = pl.program_id(1)
    xh_out = refs[13]

    @pl.when(i < seq_len // tm)
    def _():
        _pool_tile(*refs, tm=tm, seq_len=seq_len)

    @pl.when(i == seq_len // tm)
    def _():
        xh_out[...] = jnp.zeros(xh_out.shape, F32)


def _pool_tile(x_ref, xp_ref, xn_ref, g_ref, sc_ref, sh_ref, pw_ref, ps_ref, gt_ref,
               gf_ref, scf_ref, shf_ref, rw_ref, xh_out, aff_out, ext, *, tm, seq_len):
    i = pl.program_id(1)
    n_tiles = seq_len // tm
    x = x_ref[...]
    d = x.shape[1]
    grp = d // len(POOL_WINDOWS)
    h = _rms_mod(x, g_ref[...], sc_ref[...], sh_ref[...])
    hp = _rms_mod(xp_ref[...], g_ref[...], sc_ref[...], sh_ref[...])
    hn = _rms_mod(xn_ref[...], g_ref[...], sc_ref[...], sh_ref[...])
    ext[0:HALO, :] = jnp.where(i > 0, hp, 0.0)
    ext[HALO:HALO + tm, :] = h
    ext[HALO + tm:2 * HALO + tm, :] = jnp.where(i < n_tiles - 1, hn, 0.0)
    pos = i * tm + lax.broadcasted_iota(jnp.int32, (tm, 1), 0)
    outs = []
    for gi, w in enumerate(POOL_WINDOWS):
        cols = slice(gi * grp, (gi + 1) * grp)
        s = ext[HALO - w // 2:HALO - w // 2 + tm, cols]
        for j in range(-w // 2 + 1, w // 2):
            s = s + ext[HALO + j:HALO + j + tm, cols]
        cnt = (jnp.minimum(pos + w // 2, seq_len) - jnp.maximum(pos - w // 2, 0)).astype(F32)
        outs.append(_bdot(s / cnt - h[:, cols], pw_ref[gi]))
    y = jnp.concatenate(outs, axis=1) * ps_ref[...]
    _ffn_pre(x + gt_ref[...] * y, gf_ref, scf_ref, shf_ref, rw_ref, xh_out, aff_out)


def _pool_layer(x, t, g_mix, sc_a, sh_a, pool_w, pool_scale, gt_a, g_ffn, sc_f, sh_f, router_w):
    b = x.shape[0]
    d = g_mix.shape[0]
    ne = router_w.shape[1]
    ng, grp, _ = pool_w.shape
    assert max(POOL_WINDOWS) // 2 <= HALO
    tm = _row_tile(t)
    nt = t // tm
    assert _xh_rows(t, ne) == t + tm
    hb = tm // HALO
    n_halo = t // HALO

    def row(bi, i):
        return (bi, jnp.minimum(i, nt - 1), 0)

    xt = pl.BlockSpec((None, tm, d), row)
    vec = pl.BlockSpec((1, d), lambda bi, i: (0, 0))
    return pl.pallas_call(
        functools.partial(_pool_kernel, tm=tm, seq_len=t),
        grid=(b, nt + 1),
        in_specs=[xt,
                  pl.BlockSpec((None, HALO, d), lambda bi, i: (bi, jnp.maximum(i * hb - 1, 0), 0)),
                  pl.BlockSpec((None, HALO, d), lambda bi, i: (bi, jnp.minimum((i + 1) * hb, n_halo - 1), 0)),
                  vec, _mod_specs(d), _mod_specs(d),
                  pl.BlockSpec((ng, grp, grp), lambda bi, i: (0, 0, 0)), vec, _mod_specs(d),
                  vec, _mod_specs(d), _mod_specs(d), pl.BlockSpec((d, ne), lambda bi, i: (0, 0))],
        out_specs=[pl.BlockSpec((None, tm, 2 * d), lambda bi, i: (bi, i, 0)),
                   pl.BlockSpec((None, tm, ne), row)],
        out_shape=[jax.ShapeDtypeStruct((b, _xh_rows(t, ne), 2 * d), F32), jax.ShapeDtypeStruct((b, t, ne), F32)],
        scratch_shapes=[pltpu.VMEM((tm + 2 * HALO, d), F32)],
        compiler_params=_cparams(("arbitrary", "arbitrary")),
        name="pool_ffnpre",
    )(x, x, x, g_mix.reshape(1, d), sc_a, sh_a, pool_w.astype(BF16), pool_scale.reshape(1, d), gt_a,
      g_ffn.reshape(1, d), sc_f, sh_f, router_w)


def _route_threshold_kernel(aff_ref, thr_ref, *, cap):
    aff = aff_ref[...]
    ne = aff.shape[0]
    capf = float(cap)

    def bit_step(i, thr):
        cand = thr | (jnp.int32(1) << (30 - i))
        cnt = jnp.sum(jnp.where(aff >= pltpu.bitcast(cand, F32), 1.0, 0.0), axis=(1, 2), keepdims=True)
        return jnp.where(cnt >= capf, cand, thr)

    thr = lax.fori_loop(0, 31, bit_step, jnp.zeros((ne, 1, 1), jnp.int32))
    thr_ref[...] = jnp.broadcast_to(thr, thr_ref.shape)


def _route_kernel(aff_ref, thr_ref, idx_ref, gate_ref, *, cap):
    nr = aff_ref.shape[0]
    aff = aff_ref[...]
    capf = float(cap)
    thr = thr_ref[0:1, 0:1]
    gt = aff >= pltpu.bitcast(thr + 1, F32)
    eq = (aff >= pltpu.bitcast(thr, F32)) & jnp.logical_not(gt)
    need = capf - jnp.sum(jnp.where(gt, 1.0, 0.0), keepdims=True)

    ri = lax.broadcasted_iota(jnp.int32, (LANES, LANES), 0)
    ci = lax.broadcasted_iota(jnp.int32, (LANES, LANES), 1)
    upper = (ri <= ci).astype(BF16)
    rr = lax.broadcasted_iota(jnp.int32, (nr, nr), 0)
    rc = lax.broadcasted_iota(jnp.int32, (nr, nr), 1)
    lower_strict = (rc < rr).astype(BF16)
    ones8 = jnp.ones((8, LANES), BF16)
    p_iota = lax.broadcasted_iota(jnp.int32, (cap, nr), 0).astype(F32)
    r_iota = lax.broadcasted_iota(jnp.int32, (cap, nr), 1).astype(F32)
    c_iota = lax.broadcasted_iota(jnp.int32, (cap, LANES), 1).astype(F32)
    p_col = lax.broadcasted_iota(jnp.int32, (cap, 1), 0).astype(F32)
    upper_r = (rr <= rc).astype(BF16)

    eq_b = jnp.where(eq, 1.0, 0.0).astype(BF16)
    eq_rank = (jnp.dot(eq_b, upper, preferred_element_type=F32) - eq_b.astype(F32)
               + jnp.sum(jnp.dot(lower_strict, eq_b, preferred_element_type=F32), axis=1, keepdims=True))
    sel = gt | (eq & (eq_rank < need))
    sel_b = jnp.where(sel, 1.0, 0.0).astype(BF16)
    lc = jnp.dot(sel_b, upper, preferred_element_type=F32)
    row_tot = lax.dot_general(ones8, sel_b, (((1,), (1,)), ((), ())),
                              preferred_element_type=F32)[0:1, :]
    row_incl = jnp.dot(row_tot.astype(BF16), upper_r, preferred_element_type=F32)
    row_excl = row_incl - row_tot
    hit = (row_excl <= p_iota) & (p_iota < row_incl)
    hit_b = jnp.where(hit, 1.0, 0.0).astype(BF16)
    r_of_p = jnp.sum(jnp.where(hit, r_iota, 0.0), axis=1, keepdims=True)
    base = jnp.sum(jnp.where(hit, row_excl, 0.0), axis=1, keepdims=True)
    lc_row = jnp.dot(hit_b, lc.astype(BF16), preferred_element_type=F32)
    c_of_p = jnp.sum(jnp.where(lc_row <= (p_col - base), 1.0, 0.0), axis=1, keepdims=True)
    idx_ref[...] = (r_of_p * float(LANES) + c_of_p).astype(jnp.int32)
    a1 = aff.astype(BF16)
    a2 = (aff - a1.astype(F32)).astype(BF16)
    a3 = (aff - a1.astype(F32) - a2.astype(F32)).astype(BF16)
    a_row = (jnp.dot(hit_b, a1, preferred_element_type=F32) + jnp.dot(hit_b, a2, preferred_element_type=F32)
             + jnp.dot(hit_b, a3, preferred_element_type=F32))
    gate_ref[...] = jnp.sum(jnp.where(c_iota == c_of_p, a_row, 0.0), axis=1, keepdims=True)


def _route(aff):
    b, t, ne = aff.shape
    cap = _moe_cap(t, ne)
    assert t % LANES == 0
    nr = t // LANES
    aff_t = jnp.swapaxes(aff, 1, 2).reshape(b, ne, nr, LANES)
    thr = pl.pallas_call(
        functools.partial(_route_threshold_kernel, cap=cap),
        grid=(b,),
        in_specs=[pl.BlockSpec((None, ne, nr, LANES), lambda bi: (bi, 0, 0, 0))],
        out_specs=pl.BlockSpec((None, ne, 8, LANES), lambda bi: (bi, 0, 0, 0)),
        out_shape=jax.ShapeDtypeStruct((b, ne, 8, LANES), jnp.int32),
        compiler_params=_cparams(("arbitrary",)),
        name="expert_choice_threshold",
    )(aff_t)
    out_blk = pl.BlockSpec((None, None, cap, 1), lambda bi, e: (bi, e, 0, 0))
    idx, gate = pl.pallas_call(
        functools.partial(_route_kernel, cap=cap),
        grid=(b, ne),
        in_specs=[pl.BlockSpec((None, None, nr, LANES), lambda bi, e: (bi, e, 0, 0)),
                  pl.BlockSpec((None, None, 8, LANES), lambda bi, e: (bi, e, 0, 0))],
        out_specs=[out_blk, out_blk],
        out_shape=[jax.ShapeDtypeStruct((b, ne, cap, 1), jnp.int32),
                   jax.ShapeDtypeStruct((b, ne, cap, 1), F32)],
        compiler_params=_cparams(("arbitrary", "arbitrary")),
        name="expert_choice_route",
    )(aff_t, thr)
    return idx.reshape(b, ne, cap), gate


def _moe_kernel(idx_ref, gate_ref, gt_ref, xh_in, wg_ref, wu_ref, wd_ref, xh_hbm,
                buf, obuf, hbuf, gsem, osem, *, tile, n_exp, n_tiles, n_steps, d, seq_len):
    del xh_in
    bi = pl.program_id(0)
    e = pl.program_id(1)
    ti = pl.program_id(2)
    per_batch = n_exp * n_tiles
    step = bi * per_batch + e * n_tiles + ti
    slot = step % 2

    def row_in(tok, k, b_, slot_):
        return pltpu.make_async_copy(xh_hbm.at[b_, pl.ds(tok, 1)], buf.at[slot_, pl.ds(k, 1)], gsem.at[slot_])

    def row_out(tok, k, b_, slot_):
        return pltpu.make_async_copy(obuf.at[slot_, pl.ds(k, 1)], xh_hbm.at[b_, pl.ds(tok, 1), pl.ds(0, d)],
                                     osem.at[slot_])

    def wait_rows_in(slot_):
        pltpu.make_async_copy(xh_hbm.at[0, pl.ds(0, tile)], buf.at[slot_], gsem.at[slot_]).wait()

    def wait_rows_out(slot_):
        pltpu.make_async_copy(obuf.at[slot_], xh_hbm.at[0, pl.ds(0, tile), pl.ds(0, d)], osem.at[slot_]).wait()

    def fetch_now():
        def body(k2, carry):
            for p in range(DMA_PRIORITIES):
                k = k2 * DMA_PRIORITIES + p
                row_in(idx_ref[step * tile + k], k, bi, slot).start(priority=p)
            return carry

        lax.fori_loop(0, tile // DMA_PRIORITIES, body, 0, unroll=4)

    starts_expert = (ti == 0) & (e > 0)
    ends_expert = ((ti == n_tiles - 1) & (e < n_exp - 1)) | (step == n_steps - 1)

    @pl.when(starts_expert)
    def _():
        wait_rows_out(slot)
        wait_rows_out(1 - slot)
        fetch_now()

    @pl.when(step == 0)
    def _():
        obuf[...] = jnp.zeros(obuf.shape, F32)
        fetch_now()

    wait_rows_in(slot)
    hbuf[...] = buf[slot, :, d:2 * d].astype(BF16)

    src_step = jnp.where(ends_expert, step, step + 1)
    src_b = src_step // per_batch
    for k in range(tile):
        row_in(idx_ref[src_step * tile + k], k, src_b, 1 - slot).start(priority=k % DMA_PRIORITIES)
    prev_written = starts_expert | (step == 0)
    prev_step = jnp.maximum(step - 1, 0)
    prev_b = prev_step // per_batch
    for k in range(tile):
        tok = jnp.where(prev_written, seq_len + k, idx_ref[prev_step * tile + k])
        row_out(tok, k, prev_b, 1 - slot).start(priority=k % DMA_PRIORITIES)

    h = hbuf[...]
    g = jnp.dot(h, wg_ref[...], preferred_element_type=F32)
    u = jnp.dot(h, wu_ref[...], preferred_element_type=F32)
    hid = (g * jax.nn.sigmoid(g)) * u
    y = jnp.dot(hid.astype(BF16), wd_ref[...], preferred_element_type=F32)
    new_rows = buf[slot, :, 0:d] + y * gate_ref[...] * gt_ref[...]

    @pl.when((step > 0) & jnp.logical_not(starts_expert))
    def _():
        wait_rows_out(slot)

    obuf[slot] = new_rows

    @pl.when(ends_expert)
    def _():
        wait_rows_in(1 - slot)

        def put(k2, carry):
            for p in range(DMA_PRIORITIES):
                k = k2 * DMA_PRIORITIES + p
                row_out(idx_ref[step * tile + k], k, bi, slot).start(priority=p)
            return carry

        lax.fori_loop(0, tile // DMA_PRIORITIES, put, 0, unroll=4)

    @pl.when(step == n_steps - 1)
    def _():
        wait_rows_out(slot)
        wait_rows_out(1 - slot)


def _moe(xh, seq_len, idx, gate, gt_f, w_gate, w_up, w_down, layer):
    b, t_pad, d2 = xh.shape
    d = d2 // 2
    _, ne, _, ff = w_gate.shape
    cap = idx.shape[2]
    tile = _moe_tile(cap)
    assert cap == _moe_cap(seq_len, ne) and t_pad == _xh_rows(seq_len, ne)
    nt = cap // tile
    n_steps = b * ne * nt
    grid_spec = pltpu.PrefetchScalarGridSpec(
        num_scalar_prefetch=1,
        grid=(b, ne, nt),
        in_specs=[pl.BlockSpec((None, None, tile, 1), lambda bi, e, ti, idx_r: (bi, e, ti, 0)),
                  pl.BlockSpec((None, 1, d), lambda bi, e, ti, idx_r: (bi, 0, 0)),
                  pl.BlockSpec(memory_space=pl.ANY),
                  pl.BlockSpec((None, None, d, ff), lambda bi, e, ti, idx_r: (layer, e, 0, 0)),
                  pl.BlockSpec((None, None, d, ff), lambda bi, e, ti, idx_r: (layer, e, 0, 0)),
                  pl.BlockSpec((None, None, ff, d), lambda bi, e, ti, idx_r: (layer, e, 0, 0))],
        out_specs=pl.BlockSpec(memory_space=pl.ANY),
        scratch_shapes=[pltpu.VMEM((2, tile, d2), F32), pltpu.VMEM((2, tile, d), F32), pltpu.VMEM((tile, d), BF16),
                        pltpu.SemaphoreType.DMA((2,)), pltpu.SemaphoreType.DMA((2,))],
    )
    return pl.pallas_call(
        functools.partial(_moe_kernel, tile=tile, n_exp=ne, n_tiles=nt, n_steps=n_steps, d=d, seq_len=seq_len),
        grid_spec=grid_spec,
        out_shape=jax.ShapeDtypeStruct((b, t_pad, d2), F32),
        input_output_aliases={3: 0},
        compiler_params=_cparams(("arbitrary", "arbitrary", "arbitrary")),
        name="expert_ffn",
    )(idx.reshape(-1), gate, gt_f, xh, w_gate, w_up, w_down)


def _final_norm_kernel(x_ref, g_ref, o_ref):
    x = x_ref[...]
    o_ref[...] = x * lax.rsqrt(jnp.mean(x * x, axis=-1, keepdims=True) + NORM_EPS) * g_ref[...]


def _final_norm(x, t, g):
    b = x.shape[0]
    d = g.shape[0]
    tm = _tile(t, 512)
    xt = pl.BlockSpec((None, tm, d), lambda bi, i: (bi, i, 0))
    return pl.pallas_call(
        _final_norm_kernel,
        grid=(b, t // tm),
        in_specs=[xt, pl.BlockSpec((1, d), lambda bi, i: (0, 0))],
        out_specs=xt,
        out_shape=jax.ShapeDtypeStruct((b, t, d), F32),
        compiler_params=_cparams(("arbitrary", "arbitrary")),
        name="final_norm",
    )(x, g.reshape(1, d))


def _moe_layer(xh, aff, gt_f, w_gate, w_up, w_down, layer):
    idx, gate = _route(aff)
    return _moe(xh, aff.shape[1], idx, gate, gt_f, w_gate, w_up, w_down, layer)


def kernel(x, c, ctx, c_ctx, ada_w, ada_b, norm_mix, norm_ffn, w_in, shift_mu, decay_w0, decay_w2, iclr_a0,
           iclr_a2, gate_g2, k_k, k_a, r_k, ln_w, ln_b, sink, w_out, pool_w, pool_scale, router_w, exp_w_gate,
           exp_w_up, exp_w_down, norm_final):
    b, t, d = x.shape
    depth = ada_w.shape[0]
    assert depth == 2, "the context stream is only advanced for deeper stacks; not implemented"
    assert b + 1 <= 8
    cc = jnp.concatenate([c, c_ctx[None, :], jnp.zeros((8 - b - 1, d), F32)], axis=0)
    mod = _ada(cc, ada_w, ada_b)

    def mods(l, rows):
        return [mod[l, rows, None, i * d:(i + 1) * d] for i in range(6)]

    cos_t, sin_t = _rope_tables(t)
    wg_b, wu_b, wd_b = exp_w_gate.astype(BF16), exp_w_up.astype(BF16), exp_w_down.astype(BF16)
    for l in range(depth):
        sh_a, sc_a, gt_a, sh_f, sc_f, gt_f = mods(l, slice(0, b))
        if l % 2 == 0:
            e = l // 2
            csh_a, csc_a = [jnp.broadcast_to(m, (b, 1, d)) for m in mods(l, slice(b, b + 1))[:2]]
            w_p = _relayout_in_cols(w_in[e]).astype(BF16)
            mu_p = _relayout_in_cols(jnp.concatenate(
                [shift_mu[e], jnp.zeros((w_in.shape[2] - shift_mu.shape[1],), F32)]))
            mu_rkv = mu_p[None, :3 * RW]
            mu_lo = mu_p[None, COL_LORA:]
            px = _inproj(x, norm_mix[l], sc_a, sh_a, w_p, cos_t, sin_t)
            pc = _inproj(ctx, norm_mix[l], csc_a, csh_a, w_p)
            scan_args = (mu_rkv, mu_lo, decay_w0[e][:, None, :], decay_w2[e], iclr_a0[e][:, None, :], iclr_a2[e],
                         k_k[e][None], k_a[e][None], r_k[e][None])
            s0 = jnp.zeros((b, 2, HEADS // QUAD, QW, QW), F32)
            s_ctx = _rwkv_scan(pc, s0, *scan_args)[4]
            yf, yb, bf, bb, _ = _rwkv_scan(px, s_ctx, *scan_args)
            g2_pad = jnp.zeros((LORA_W, RW), F32).at[LORA_GATE:LORA_GATE + GATE_LORA].set(gate_g2[e]).astype(BF16)
            rwkv_x = _rwkv_finish(yf, yb, bf, bb, px, mu_lo, g2_pad, ln_w[e][None], ln_b[e][None])
            att_x = _attention(px, pc, sink[e])
            xh, aff = _outproj(rwkv_x, att_x, w_out[e].astype(BF16), x, gt_a, norm_ffn[l], sc_f, sh_f,
                               router_w[l].astype(BF16))
        else:
            o = l // 2
            xh, aff = _pool_layer(x, t, norm_mix[l], sc_a, sh_a, pool_w[o], pool_scale[o], gt_a, norm_ffn[l],
                                  sc_f, sh_f, router_w[l].astype(BF16))
        x = _moe_layer(xh, aff, gt_f, wg_b, wu_b, wd_b, l)
    return _final_norm(x, t, norm_final)
```
